```python
import math
import jax
import jax.numpy as jnp
from jax import lax
import numpy as np

D_MODEL = 2048
BATCH = 2
SEQ = 4096
DEPTH = 4

MIX_WIDTH = D_MODEL
SWA_WIDTH = MIX_WIDTH // 2
DN_WIDTH = MIX_WIDTH - SWA_WIDTH
HEAD_DIM = 64
N_Q_HEADS = SWA_WIDTH // HEAD_DIM
KV_GROUP = 8
N_KV_HEADS = N_Q_HEADS // KV_GROUP
WINDOW = 128
BLOCK = 128
ROPE_THETA = 10000.0
DN_HEAD_DIM = 128
DN_HEADS = DN_WIDTH // DN_HEAD_DIM
CONV_WIDTH = 4
CHUNK = 64
N_EXPERTS = 16
N_GROUPS = 4
EXPERTS_PER_GROUP = N_EXPERTS // N_GROUPS
TOP_K = 2
D_EXPERT = D_MODEL // 4
ALPHA = (2 * DEPTH) ** 0.25
BETA_INIT = (8 * DEPTH) ** -0.25
LN_EPS = 1e-5
RMS_EPS = 1e-6
SWA_KV_DIM = N_KV_HEADS * HEAD_DIM
DN_QKV_DIM = 3 * DN_WIDTH
PROJ_SIZES = (SWA_WIDTH, SWA_KV_DIM, SWA_KV_DIM, DN_QKV_DIM, DN_WIDTH, DN_HEADS, DN_HEADS)
PROJ_DIM = sum(PROJ_SIZES)

kernel_name = "hymba_swa_sink_gdn_grouped_moe_deepnorm"


def layer_norm(x, g, b):
    xf = x.astype(jnp.float32)
    mu = jnp.mean(xf, axis=-1, keepdims=True)
    xc = xf - mu
    var = jnp.mean(xc * xc, axis=-1, keepdims=True)
    return (xc * lax.rsqrt(var + LN_EPS) * g.astype(jnp.float32) + b.astype(jnp.float32)).astype(x.dtype)


def rope_tables(positions):
    inv_freq = 1.0 / (ROPE_THETA ** (jnp.arange(0, HEAD_DIM, 2, dtype=jnp.float32) / HEAD_DIM))
    ang = positions.astype(jnp.float32)[..., None] * inv_freq
    return jnp.cos(ang), jnp.sin(ang)


def apply_rope(x, cos, sin):
    xf = x.astype(jnp.float32)
    x1, x2 = jnp.split(xf, 2, axis=-1)
    c = cos[:, :, None, :]
    s = sin[:, :, None, :]
    return jnp.concatenate([x1 * c - x2 * s, x2 * c + x1 * s], axis=-1).astype(x.dtype)


def sliding_window_attention(q, k, v, sinks):
    B, S = q.shape[0], q.shape[1]
    nb = S // BLOCK
    qb = q.reshape(B, nb, BLOCK, N_KV_HEADS, KV_GROUP, HEAD_DIM)
    pad = ((0, 0), (BLOCK, 0), (0, 0), (0, 0))
    kp = jnp.pad(k, pad).reshape(B, nb + 1, BLOCK, N_KV_HEADS, HEAD_DIM)
    vp = jnp.pad(v, pad).reshape(B, nb + 1, BLOCK, N_KV_HEADS, HEAD_DIM)
    kb = jnp.concatenate([kp[:, :-1], kp[:, 1:]], axis=2)
    vb = jnp.concatenate([vp[:, :-1], vp[:, 1:]], axis=2)
    s = jnp.einsum('bnqhgd,bnkhd->bnhgqk', qb, kb).astype(jnp.float32) * (HEAD_DIM ** -0.5)
    qpos = jnp.arange(BLOCK)[:, None] + BLOCK
    kpos = jnp.arange(2 * BLOCK)[None, :]
    rel = qpos - kpos
    band = (rel >= 0) & (rel < WINDOW)
    not_pad = (jnp.arange(nb)[:, None, None] > 0) | (kpos >= BLOCK)[None]
    valid = band[None] & not_pad
    s = jnp.where(valid[None, :, None, None], s, -jnp.inf)
    sink = jnp.broadcast_to(
        sinks.astype(jnp.float32).reshape(N_KV_HEADS, KV_GROUP)[None, None, :, :, None, None],
        s.shape[:-1] + (1,))
    p = jax.nn.softmax(jnp.concatenate([s, sink], axis=-1), axis=-1)[..., :-1]
    o = jnp.einsum('bnhgqk,bnkhd->bnqhgd', p.astype(v.dtype), vb)
    return o.reshape(B, S, N_Q_HEADS * HEAD_DIM)


def causal_depthwise_conv(x, w):
    C = x.shape[-1]
    return lax.conv_general_dilated(
        x, w[:, None, :].astype(x.dtype), window_strides=(1,),
        padding=[(CONV_WIDTH - 1, 0)], dimension_numbers=('NWC', 'WIO', 'NWC'),
        feature_group_count=C)


def l2norm(x):
    return x * lax.rsqrt(jnp.sum(x * x, axis=-1, keepdims=True) + RMS_EPS)


def gated_delta_rule(q, k, v, g, beta):
    B, S, H, DK = q.shape
    DV = v.shape[-1]
    n = S // CHUNK

    def chunk(t):
        return t.reshape(B, n, CHUNK, H, -1).transpose(0, 3, 1, 2, 4)

    q, k, v = chunk(q), chunk(k), chunk(v)
    g = g.reshape(B, n, CHUNK, H).transpose(0, 3, 1, 2)
    beta = beta.reshape(B, n, CHUNK, H).transpose(0, 3, 1, 2)
    gc = jnp.cumsum(g, axis=-1)
    idx = jnp.arange(CHUNK)
    causal = idx[:, None] >= idx[None, :]
    strict = idx[:, None] > idx[None, :]
    decay = jnp.exp(jnp.where(causal, gc[..., :, None] - gc[..., None, :], -jnp.inf))
    kb = k * beta[..., None]
    vb = v * beta[..., None]
    L = jnp.where(strict, jnp.einsum('bhncd,bhnsd->bhncs', kb, k) * decay, 0.0)
    eye = jnp.eye(CHUNK, dtype=L.dtype)
    T = lax.linalg.triangular_solve(eye + L, jnp.broadcast_to(eye, L.shape),
                                    left_side=True, lower=True, unit_diagonal=True)
    u = jnp.einsum('bhncs,bhnsv->bhncv', T, vb)
    w = jnp.einsum('bhncs,bhnsk->bhnck', T, kb * jnp.exp(gc)[..., None])
    a_qk = jnp.where(causal, jnp.einsum('bhncd,bhnsd->bhncs', q, k) * decay, 0.0)
    qg = q * jnp.exp(gc)[..., None]
    kd = k * jnp.exp(gc[..., -1:] - gc)[..., None]
    glast = jnp.exp(gc[..., -1])
    xs = tuple(jnp.moveaxis(t, 2, 0) for t in (u, w, a_qk, qg, kd, glast))

    def step(state, inp):
        u_i, w_i, a_i, qg_i, kd_i, gl_i = inp
        v_new = u_i - jnp.einsum('bhck,bhkv->bhcv', w_i, state)
        o_i = jnp.einsum('bhck,bhkv->bhcv', qg_i, state) + jnp.einsum('bhcs,bhsv->bhcv', a_i, v_new)
        state = state * gl_i[..., None, None] + jnp.einsum('bhck,bhcv->bhkv', kd_i, v_new)
        return state, o_i

    s0 = jnp.zeros((B, H, DK, DV), jnp.float32)
    _, o = lax.scan(step, s0, xs)
    return o.transpose(1, 0, 3, 2, 4).reshape(B, S, H, DV)


def hybrid_mixer(h, w_in, sinks, conv_w, a_log, dt_bias, dn_norm_w, w_out, cos, sin):
    B, S, _ = h.shape
    proj = h @ w_in
    cuts = []
    off = 0
    for size in PROJ_SIZES[:-1]:
        off += size
        cuts.append(off)
    sq, sk, sv, dn_qkv, dn_z, dn_a, dn_b = jnp.split(proj, cuts, axis=-1)

    sq = apply_rope(sq.reshape(B, S, N_Q_HEADS, HEAD_DIM), cos, sin)
    sk = apply_rope(sk.reshape(B, S, N_KV_HEADS, HEAD_DIM), cos, sin)
    sv = sv.reshape(B, S, N_KV_HEADS, HEAD_DIM)
    attn_out = sliding_window_attention(sq, sk, sv, sinks)

    dn_qkv = jax.nn.silu(causal_depthwise_conv(dn_qkv, conv_w)).astype(jnp.float32)
    dq, dk, dv = jnp.split(dn_qkv, 3, axis=-1)
    dq = l2norm(dq.reshape(B, S, DN_HEADS, DN_HEAD_DIM)) * (DN_HEAD_DIM ** -0.5)
    dk = l2norm(dk.reshape(B, S, DN_HEADS, DN_HEAD_DIM))
    dv = dv.reshape(B, S, DN_HEADS, DN_HEAD_DIM)
    beta = jax.nn.sigmoid(dn_b.astype(jnp.float32))
    g = -jnp.exp(a_log.astype(jnp.float32)) * jax.nn.softplus(
        dn_a.astype(jnp.float32) + dt_bias.astype(jnp.float32))
    o = gated_delta_rule(dq, dk, dv, g, beta)
    z = dn_z.reshape(B, S, DN_HEADS, DN_HEAD_DIM).astype(jnp.float32)
    o = (o * lax.rsqrt(jnp.mean(o * o, axis=-1, keepdims=True) + RMS_EPS)
         * dn_norm_w.astype(jnp.float32) * jax.nn.silu(z))
    dn_out = o.reshape(B, S, DN_WIDTH).astype(h.dtype)

    return jnp.concatenate([attn_out, dn_out], axis=-1) @ w_out


def grouped_moe(h, router_w, router_bias, w_gate, w_up, w_down):
    B, S, D = h.shape
    t = h.reshape(-1, D)
    scores = jax.nn.sigmoid((t @ router_w).astype(jnp.float32))
    sel = scores + router_bias.astype(jnp.float32)
    grp_top = lax.top_k(sel.reshape(-1, N_GROUPS, EXPERTS_PER_GROUP), 2)[0]
    gidx = jnp.argmax(jnp.sum(grp_top, axis=-1), axis=-1)
    in_group = (jnp.arange(N_EXPERTS) // EXPERTS_PER_GROUP)[None, :] == gidx[:, None]
    _, eidx = lax.top_k(jnp.where(in_group, sel, -jnp.inf), TOP_K)
    wsel = jnp.take_along_axis(scores, eidx, axis=-1)
    wsel = wsel / jnp.sum(wsel, axis=-1, keepdims=True)
    gates = jnp.sum(jax.nn.one_hot(eidx, N_EXPERTS, dtype=jnp.float32) * wsel[..., None], axis=1)
    a = jnp.einsum('td,edf->etf', t, w_gate)
    u = jnp.einsum('td,edf->etf', t, w_up)
    hidden = jax.nn.silu(a) * u * gates.T.astype(t.dtype)[..., None]
    out = jnp.einsum('etf,efd->td', hidden, w_down)
    return out.reshape(B, S, D)


def setup_inputs(seed: int = 0) -> dict:
    key = jax.random.key(seed)
    ks = jax.random.split(key, 20)
    f32 = jnp.float32

    def nrm(k, shape, scale):
        return jax.random.normal(k, shape, f32) * scale

    x = nrm(ks[0], (BATCH, SEQ, D_MODEL), 1.0)
    positions = jnp.broadcast_to(jnp.arange(SEQ, dtype=jnp.int32), (BATCH, SEQ))
    w_in = nrm(ks[1], (DEPTH, D_MODEL, PROJ_DIM), D_MODEL ** -0.5)
    attn_sinks = nrm(ks[2], (DEPTH, N_Q_HEADS), 0.5)
    conv_w = nrm(ks[3], (DEPTH, CONV_WIDTH, DN_QKV_DIM), CONV_WIDTH ** -0.5)
    a_log = jnp.log(jax.random.uniform(ks[4], (DEPTH, DN_HEADS), f32, 1.0, 16.0))
    dt = jnp.exp(jax.random.uniform(ks[5], (DEPTH, DN_HEADS), f32, math.log(1e-3), math.log(1e-1)))
    dt_bias = dt + jnp.log(-jnp.expm1(-dt))
    dn_norm_w = 1.0 + nrm(ks[6], (DEPTH, DN_HEAD_DIM), 0.02)
    w_out = nrm(ks[7], (DEPTH, MIX_WIDTH, D_MODEL), MIX_WIDTH ** -0.5 * BETA_INIT)
    ln1_g = 1.0 + nrm(ks[8], (DEPTH, D_MODEL), 0.02)
    ln1_b = nrm(ks[9], (DEPTH, D_MODEL), 0.02)
    router_w = nrm(ks[10], (D_MODEL, N_EXPERTS), D_MODEL ** -0.5)
    router_bias = nrm(ks[11], (N_EXPERTS,), 0.01)
    w_gate = nrm(ks[12], (DEPTH, N_EXPERTS, D_MODEL, D_EXPERT), D_MODEL ** -0.5)
    w_up = nrm(ks[13], (DEPTH, N_EXPERTS, D_MODEL, D_EXPERT), D_MODEL ** -0.5)
    w_down = nrm(ks[14], (DEPTH, N_EXPERTS, D_EXPERT, D_MODEL), D_EXPERT ** -0.5 * BETA_INIT)
    ln2_g = 1.0 + nrm(ks[15], (DEPTH, D_MODEL), 0.02)
    ln2_b = nrm(ks[16], (DEPTH, D_MODEL), 0.02)
    return {"x": x, "positions": positions, "w_in": w_in, "attn_sinks": attn_sinks,
            "conv_w": conv_w, "a_log": a_log, "dt_bias": dt_bias, "dn_norm_w": dn_norm_w,
            "w_out": w_out, "ln1_g": ln1_g, "ln1_b": ln1_b, "router_w": router_w,
            "router_bias": router_bias, "w_gate": w_gate, "w_up": w_up, "w_down": w_down,
            "ln2_g": ln2_g, "ln2_b": ln2_b}


def reference(x, positions, w_in, attn_sinks, conv_w, a_log, dt_bias, dn_norm_w, w_out,
              ln1_g, ln1_b, router_w, router_bias, w_gate, w_up, w_down, ln2_g, ln2_b):
    cos, sin = rope_tables(positions)
    for l in range(DEPTH):
        mix = hybrid_mixer(x, w_in[l], attn_sinks[l], conv_w[l], a_log[l], dt_bias[l],
                           dn_norm_w[l], w_out[l], cos, sin)
        x = layer_norm(ALPHA * x + mix, ln1_g[l], ln1_b[l])
        ffn = grouped_moe(x, router_w, router_bias, w_gate[l], w_up[l], w_down[l])
        x = layer_norm(ALPHA * x + ffn, ln2_g[l], ln2_b[l])
    return x
```

```python
import functools

import numpy as np
import jax
import jax.numpy as jnp
from jax import lax
from jax.experimental import pallas as pl
from jax.experimental.pallas import tpu as pltpu

F32 = jnp.float32
BF16 = jnp.bfloat16
HIGHEST = lax.Precision.HIGHEST

D_MODEL = 2048
HEAD_DIM = 64
N_Q_HEADS = 16
KV_GROUP = 8
N_KV_HEADS = 2
ATT_BLOCK = 128
ROPE_THETA = 10000.0
DN_HEADS = 8
DN_HEAD_DIM = 128
CONV_WIDTH = 4
CHUNK = 64
SUPER = 256
N_EXPERTS = 16
N_GROUPS = 4
EXPERTS_PER_GROUP = 4
D_EXPERT = 512
LN_EPS = 1e-5
RMS_EPS = 1e-6
NEG = -1e30

LANE = 128
Q_BLK, K_BLK, V_BLK = 0, 8, 9
DNQ_BLK, DNK_BLK, DNV_BLK, Z_BLK = 10, 18, 26, 34
MAIN_COLS = 42 * LANE
MIB = 1024 * 1024


def _params(sem, vmem_mib):
    return pltpu.CompilerParams(dimension_semantics=sem, vmem_limit_bytes=vmem_mib * MIB)


def _inproj_kernel(x_ref, w_ref, o_ref):
    o_ref[...] = jnp.dot(x_ref[...].astype(BF16), w_ref[...], preferred_element_type=F32)


def _inproj(x2d, w_main, l):
    T, K = x2d.shape
    tm, tn = 512, 768
    return pl.pallas_call(
        _inproj_kernel,
        grid=(T // tm, MAIN_COLS // tn),
        in_specs=[pl.BlockSpec((tm, K), lambda i, j: (i, 0)),
                  pl.BlockSpec((None, K, tn), lambda i, j: (l, 0, j))],
        out_specs=pl.BlockSpec((tm, tn), lambda i, j: (i, j)),
        out_shape=jax.ShapeDtypeStruct((T, MAIN_COLS), F32),
        compiler_params=_params(("parallel", "arbitrary"), 40),
        name="inproj",
    )(x2d, w_main)


def _tail_kernel(x_ref, w_ref, o_ref):
    o_ref[...] = jnp.dot(x_ref[...], w_ref[...], preferred_element_type=F32, precision=HIGHEST)


def _tailproj(x2d, w_tail, l):
    T, K = x2d.shape
    tm = 512
    return pl.pallas_call(
        _tail_kernel,
        grid=(T // tm,),
        in_specs=[pl.BlockSpec((tm, K), lambda i: (i, 0)),
                  pl.BlockSpec((None, K, LANE), lambda i: (l, 0, 0))],
        out_specs=pl.BlockSpec((tm, LANE), lambda i: (i, 0)),
        out_shape=jax.ShapeDtypeStruct((T, LANE), F32),
        compiler_params=_params(("parallel",), 32),
        name="tailproj",
    )(x2d, w_tail)


def _rope_kernel(pos_ref, inv_ref, cos_ref, sin_ref):
    ang = pos_ref[...].astype(F32) * inv_ref[...]
    lane = lax.broadcasted_iota(jnp.int32, ang.shape, 1)
    sign = jnp.where((lane & (HEAD_DIM - 1)) < HEAD_DIM // 2, -1.0, 1.0)
    cos_ref[...] = jnp.cos(ang)
    sin_ref[...] = jnp.sin(ang) * sign


def _rope_tables(positions):
    T = positions.size
    tm = min(1024, T)
    half = HEAD_DIM // 2
    inv_freq = 1.0 / (ROPE_THETA ** (jnp.arange(0, HEAD_DIM, 2, dtype=F32) / HEAD_DIM))
    inv = jnp.tile(inv_freq, LANE // half).reshape(1, LANE)
    return pl.pallas_call(
        _rope_kernel,
        grid=(T // tm,),
        in_specs=[pl.BlockSpec((tm, 1), lambda i: (i, 0)),
                  pl.BlockSpec((1, LANE), lambda i: (0, 0))],
        out_specs=[pl.BlockSpec((tm, LANE), lambda i: (i, 0))] * 2,
        out_shape=[jax.ShapeDtypeStruct((T, LANE), F32)] * 2,
        compiler_params=_params(("parallel",), 32),
        name="rope_tables",
    )(positions.reshape(T, 1), inv)


def _rope(x, cos, sin):
    lane = lax.broadcasted_iota(jnp.int32, x.shape, 1)
    first_half = (lane & (HEAD_DIM - 1)) < HEAD_DIM // 2
    swapped = jnp.where(first_half,
                        pltpu.roll(x, LANE - HEAD_DIM // 2, axis=1),
                        pltpu.roll(x, HEAD_DIM // 2, axis=1))
    return x * cos + swapped * sin


def _swa_kernel(sink_ref, q_ref, kc_ref, kp_ref, vc_ref, vp_ref, cc_ref, sc_ref, cp_ref, sp_ref, o_ref):
    blk = pl.program_id(1)
    cos_c, sin_c = cc_ref[...], sc_ref[...]
    k_cur = _rope(kc_ref[...], cos_c, sin_c)
    k_prev = _rope(kp_ref[...], cp_ref[...], sp_ref[...])
    k_cat = jnp.concatenate([k_prev, k_cur], axis=0).astype(BF16)
    v_cat = jnp.concatenate([vp_ref[...], vc_ref[...]], axis=0).astype(BF16)

    qi = lax.broadcasted_iota(jnp.int32, (ATT_BLOCK, 2 * ATT_BLOCK), 0)
    kj = lax.broadcasted_iota(jnp.int32, (ATT_BLOCK, 2 * ATT_BLOCK), 1)
    first_key = jnp.where(blk > 0, 0, ATT_BLOCK)
    valid = (kj > qi) & (kj <= qi + ATT_BLOCK) & (kj >= first_key)

    outs = []
    for pair in range(N_Q_HEADS // 2):
        qp = _rope(q_ref[:, pair * LANE:(pair + 1) * LANE], cos_c, sin_c) * (HEAD_DIM ** -0.5)
        qp = qp.astype(BF16)
        for sub in range(2):
            hd = 2 * pair + sub
            kvh = hd // KV_GROUP
            q_h = qp[:, sub * HEAD_DIM:(sub + 1) * HEAD_DIM]
            k_h = k_cat[:, kvh * HEAD_DIM:(kvh + 1) * HEAD_DIM]
            v_h = v_cat[:, kvh * HEAD_DIM:(kvh + 1) * HEAD_DIM]
            s = lax.dot_general(q_h, k_h, (((1,), (1,)), ((), ())), preferred_element_type=F32)
            s = jnp.where(valid, s, NEG)
            sink = sink_ref[hd]
            m = jnp.maximum(jnp.max(s, axis=-1, keepdims=True), sink)
            p = jnp.exp(s - m)
            denom = jnp.sum(p, axis=-1, keepdims=True) + jnp.exp(sink - m)
            o = jnp.dot(p.astype(BF16), v_h, preferred_element_type=F32)
            outs.append(o / denom)
    o_ref[...] = jnp.concatenate(outs, axis=1).astype(o_ref.dtype)


def _swa(proj, cos, sin, sinks, batch, seq):
    T = proj.shape[0]
    nb = seq // ATT_BLOCK
    cur = lambda b, i: b * nb + i
    prev = lambda b, i: b * nb + jnp.maximum(i - 1, 0)
    blk = lambda rows_fn, col: pl.BlockSpec((ATT_BLOCK, LANE), lambda b, i: (rows_fn(b, i), col))
    return pl.pallas_call(
        _swa_kernel,
        grid=(batch, nb),
        in_specs=[pl.BlockSpec(memory_space=pltpu.SMEM),
                  pl.BlockSpec((ATT_BLOCK, N_Q_HEADS * HEAD_DIM), lambda b, i: (cur(b, i), Q_BLK)),
                  blk(cur, K_BLK), blk(prev, K_BLK), blk(cur, V_BLK), blk(prev, V_BLK),
                  blk(cur, 0), blk(cur, 0), blk(prev, 0), blk(prev, 0)],
        out_specs=pl.BlockSpec((ATT_BLOCK, N_Q_HEADS * HEAD_DIM), lambda b, i: (cur(b, i), 0)),
        out_shape=jax.ShapeDtypeStruct((T, N_Q_HEADS * HEAD_DIM), BF16),
        compiler_params=_params(("parallel", "arbitrary"), 32),
        name="swa",
    )(sinks, proj, proj, proj, proj, proj, cos, sin, cos, sin)


def _delta_masks():
    r = np.arange(SUPER)[:, None]
    c = np.arange(SUPER)[None, :]
    same = (r // CHUNK) == (c // CHUNK)
    ms = [same & (r > c), same & (r >= c), ((r // 4) == (c // 4)) & (r > c)]
    s = 4
    while s < CHUNK:
        ms.append(((r // (2 * s)) == (c // (2 * s))) & ((r // s) % 2 == 1) & ((c // s) % 2 == 0))
        s *= 2
    return jnp.asarray(np.stack(ms).astype(np.float32))


def _softplus(x):
    return jnp.maximum(x, 0.0) + jnp.log(1.0 + jnp.exp(-jnp.abs(x)))


def _bdot(a, b):
    return jnp.dot(a.astype(BF16), b.astype(BF16), preferred_element_type=F32)


def _gdn_kernel(q_ref, k_ref, v_ref, hq_ref, hk_ref, hv_ref, cwq_ref, cwk_ref, cwv_ref,
                z_ref, tail_ref, tailt_ref, alog_r_ref, dtb_r_ref, alog_c_ref, dtb_c_ref,
                nw_ref, mask_ref, o_ref, state_ref):
    h = pl.program_id(1)
    step = pl.program_id(2)

    @pl.when(step == 0)
    def _():
        state_ref[...] = jnp.zeros_like(state_ref)

    def conv_silu(x_ref, halo_ref, w_ref):
        halo = jnp.where(step > 0, halo_ref[...], 0.0)
        xx = jnp.concatenate([halo, x_ref[...]], axis=0)
        w = w_ref[...]
        y = xx[8:8 + SUPER] * w[3:4]
        for j in range(CONV_WIDTH - 1):
            off = 8 - (CONV_WIDTH - 1) + j
            y = y + xx[off:off + SUPER] * w[j:j + 1]
        return y * jax.nn.sigmoid(y)

    def l2norm(x):
        return x * lax.rsqrt(jnp.sum(x * x, axis=-1, keepdims=True) + RMS_EPS)

    q = l2norm(conv_silu(q_ref, hq_ref, cwq_ref)) * (DN_HEAD_DIM ** -0.5)
    k = l2norm(conv_silu(k_ref, hk_ref, cwk_ref))
    v = conv_silu(v_ref, hv_ref, cwv_ref)

    tail = tail_ref[...]
    lane = lax.broadcasted_iota(jnp.int32, tail.shape, 1)
    g_all = -jnp.exp(alog_r_ref[...]) * _softplus(tail + dtb_r_ref[...])
    g_col = jnp.sum(jnp.where(lane == h, g_all, 0.0), axis=-1, keepdims=True)
    beta = jax.nn.sigmoid(jnp.sum(jnp.where(lane == DN_HEADS + h, tail, 0.0), axis=-1, keepdims=True))
    tail_t = tailt_ref[...]
    sub = lax.broadcasted_iota(jnp.int32, tail_t.shape, 0)
    g_rows = -jnp.exp(alog_c_ref[...]) * _softplus(tail_t + dtb_c_ref[...])
    g_row = jnp.sum(jnp.where(sub == h, g_rows, 0.0), axis=0, keepdims=True)

    m_strict, m_causal = mask_ref[0], mask_ref[1]
    gc_col = jnp.dot(m_causal, jnp.broadcast_to(g_col, (SUPER, LANE)),
                     preferred_element_type=F32, precision=HIGHEST)[:, :1]
    gc_row = lax.dot_general(jnp.broadcast_to(g_row, (8, SUPER)), m_causal, (((1,), (1,)), ((), ())),
                             preferred_element_type=F32, precision=HIGHEST)[:1]
    decay = jnp.exp(jnp.where(m_causal > 0, gc_col - gc_row, NEG))

    kb = k * beta
    vb = v * beta
    kq = lax.dot_general(jnp.concatenate([kb, q], axis=0).astype(BF16), k.astype(BF16),
                         (((1,), (1,)), ((), ())), preferred_element_type=F32)
    lmat = kq[:SUPER] * decay * m_strict
    a_qk = kq[SUPER:] * decay

    ri = lax.broadcasted_iota(jnp.int32, (SUPER, SUPER), 0)
    ci = lax.broadcasted_iota(jnp.int32, (SUPER, SUPER), 1)
    eye = jnp.where(ri == ci, 1.0, 0.0)
    l4 = lmat * mask_ref[2]
    l4sq = _bdot(l4, l4)
    tinv = eye - l4 + l4sq - _bdot(l4, l4sq)
    for lvl in range(4):
        c_off = lmat * mask_ref[3 + lvl]
        tinv = tinv - _bdot(tinv, _bdot(c_off, tinv))

    e_col = jnp.exp(gc_col)
    uw = _bdot(tinv, jnp.concatenate([vb, kb * e_col], axis=1))
    u, w = uw[:, :LANE], uw[:, LANE:]
    qg = q * e_col

    lane_r = lax.broadcasted_iota(jnp.int32, (1, SUPER), 1)
    gc_last_row = jnp.zeros((1, SUPER), F32)
    for c in range(SUPER // CHUNK):
        last = gc_row[:, (c + 1) * CHUNK - 1:(c + 1) * CHUNK]
        gc_last_row = jnp.where((lane_r >> 6) == c, last, gc_last_row)
    kd_t = k.T * jnp.exp(gc_last_row - gc_row)

    state = state_ref[...]
    outs = []
    for c in range(SUPER // CHUNK):
        r0 = c * CHUNK
        ws = _bdot(jnp.concatenate([w[r0:r0 + CHUNK], qg[r0:r0 + CHUNK]], axis=0), state)
        v_new = u[r0:r0 + CHUNK] - ws[:CHUNK]
        outs.append(ws[CHUNK:] + _bdot(a_qk[r0:r0 + CHUNK, r0:r0 + CHUNK], v_new))
        g_last = jnp.exp(gc_row[:, r0 + CHUNK - 1:r0 + CHUNK])
        state = state * g_last + _bdot(kd_t[:, r0:r0 + CHUNK], v_new)
    state_ref[...] = state

    o = jnp.concatenate(outs, axis=0)
    z = z_ref[...]
    o = o * lax.rsqrt(jnp.mean(o * o, axis=-1, keepdims=True) + RMS_EPS) * nw_ref[...] * (z * jax.nn.sigmoid(z))
    o_ref[...] = o.astype(o_ref.dtype)


def _gdn(proj, tail, tail_t, conv_w, alog_r, dtb_r, alog_c, dtb_c, norm_w, masks, batch, seq):
    T = proj.shape[0]
    ns = seq // SUPER
    halo_per_super = SUPER // 8
    row = lambda b, i: b * ns + i
    halo_row = lambda b, i: jnp.maximum((b * ns + i) * halo_per_super - 1, 0)
    cur = lambda col: pl.BlockSpec((SUPER, LANE), lambda b, h, i: (row(b, i), col + h))
    halo = lambda col: pl.BlockSpec((8, LANE), lambda b, h, i: (halo_row(b, i), col + h))
    cw = lambda col: pl.BlockSpec((CONV_WIDTH, LANE), lambda b, h, i: (0, col + h))
    const2 = lambda shape: pl.BlockSpec(shape, lambda b, h, i: (0, 0))
    return pl.pallas_call(
        _gdn_kernel,
        grid=(batch, DN_HEADS, ns),
        in_specs=[cur(DNQ_BLK), cur(DNK_BLK), cur(DNV_BLK),
                  halo(DNQ_BLK), halo(DNK_BLK), halo(DNV_BLK),
                  cw(0), cw(DN_HEADS), cw(2 * DN_HEADS),
                  cur(Z_BLK),
                  pl.BlockSpec((SUPER, LANE), lambda b, h, i: (row(b, i), 0)),
                  pl.BlockSpec((16, SUPER), lambda b, h, i: (0, row(b, i))),
                  const2((1, LANE)), const2((1, LANE)), const2((16, 1)), const2((16, 1)),
                  const2((1, LANE)),
                  pl.BlockSpec(masks.shape, lambda b, h, i: (0, 0, 0))],
        out_specs=pl.BlockSpec((SUPER, LANE), lambda b, h, i: (row(b, i), h)),
        out_shape=jax.ShapeDtypeStruct((T, DN_HEADS * DN_HEAD_DIM), BF16),
        scratch_shapes=[pltpu.VMEM((DN_HEAD_DIM, DN_HEAD_DIM), F32)],
        compiler_params=_params(("parallel", "parallel", "arbitrary"), 32),
        name="gdn",
    )(proj, proj, proj, proj, proj, proj, conv_w, conv_w, conv_w, proj, tail, tail_t,
      alog_r, dtb_r, alog_c, dtb_c, norm_w, masks)


def _layer_norm(y, g, b):
    mu = jnp.mean(y, axis=-1, keepdims=True)
    yc = y - mu
    var = jnp.mean(yc * yc, axis=-1, keepdims=True)
    return yc * lax.rsqrt(var + LN_EPS) * g + b


def _outproj_kernel(alpha, a_ref, d_ref, wa_ref, wd_ref, x_ref, g_ref, b_ref, o_ref):
    mix = jnp.dot(a_ref[...], wa_ref[...], preferred_element_type=F32)
    mix = mix + jnp.dot(d_ref[...], wd_ref[...], preferred_element_type=F32)
    o_ref[...] = _layer_norm(alpha * x_ref[...] + mix, g_ref[...], b_ref[...])


def _outproj_ln(attn, dn, w_out, x2d, g, b, l, alpha):
    T, D = x2d.shape
    tm = 256
    half = attn.shape[1]
    return pl.pallas_call(
        functools.partial(_outproj_kernel, alpha),
        grid=(T // tm,),
        in_specs=[pl.BlockSpec((tm, half), lambda i: (i, 0)),
                  pl.BlockSpec((tm, half), lambda i: (i, 0)),
                  pl.BlockSpec((None, half, D), lambda i: (l, 0, 0)),
                  pl.BlockSpec((None, half, D), lambda i: (l, 1, 0)),
                  pl.BlockSpec((tm, D), lambda i: (i, 0)),
                  pl.BlockSpec((None, 1, D), lambda i: (l, 0, 0)),
                  pl.BlockSpec((None, 1, D), lambda i: (l, 0, 0))],
        out_specs=pl.BlockSpec((tm, D), lambda i: (i, 0)),
        out_shape=jax.ShapeDtypeStruct((T, D), F32),
        compiler_params=_params(("parallel",), 48),
        name="outproj_ln",
    )(attn, dn, w_out, w_out, x2d, g, b)


def _router_kernel(x_ref, w_ref, bias_ref, gates_ref):
    logits = jnp.dot(x_ref[...], w_ref[...], preferred_element_type=F32, precision=HIGHEST)
    scores = jax.nn.sigmoid(logits)
    sel = scores + bias_ref[...]
    e = lax.broadcasted_iota(jnp.int32, sel.shape, 1)
    big = jnp.int32(N_EXPERTS)

    def top2(vals):
        m1 = jnp.max(vals, axis=-1, keepdims=True)
        i1 = jnp.min(jnp.where(vals == m1, e, big), axis=-1, keepdims=True)
        rest = jnp.where(e == i1, -jnp.inf, vals)
        m2 = jnp.max(rest, axis=-1, keepdims=True)
        i2 = jnp.min(jnp.where(rest == m2, e, big), axis=-1, keepdims=True)
        return m1, i1, m2, i2

    best = None
    for grp in range(N_GROUPS):
        m1, i1, m2, i2 = top2(jnp.where((e >> 2) == grp, sel, -jnp.inf))
        gsum = m1 + m2
        if best is None:
            best = (gsum, i1, i2)
        else:
            take = gsum > best[0]
            best = (jnp.where(take, gsum, best[0]), jnp.where(take, i1, best[1]), jnp.where(take, i2, best[2]))
    _, e1, e2 = best
    w1 = jnp.sum(jnp.where(e == e1, scores, 0.0), axis=-1, keepdims=True)
    w2 = jnp.sum(jnp.where(e == e2, scores, 0.0), axis=-1, keepdims=True)
    tot = w1 + w2
    gates_ref[...] = jnp.where(e == e1, w1 / tot, 0.0) + jnp.where(e == e2, w2 / tot, 0.0)


def _router(x2d, router_w, router_bias):
    T, D = x2d.shape
    tm = 512
    return pl.pallas_call(
        _router_kernel,
        grid=(T // tm,),
        in_specs=[pl.BlockSpec((tm, D), lambda i: (i, 0)),
                  pl.BlockSpec((D, N_EXPERTS), lambda i: (0, 0)),
                  pl.BlockSpec((1, N_EXPERTS), lambda i: (0, 0))],
        out_specs=pl.BlockSpec((tm, N_EXPERTS), lambda i: (i, 0)),
        out_shape=jax.ShapeDtypeStruct((T, N_EXPERTS), F32),
        compiler_params=_params(("parallel",), 32),
        name="router",
    )(x2d, router_w, router_bias.reshape(1, N_EXPERTS))


def _moe_kernel(alpha, x_ref, gates_ref, wg_ref, wu_ref, wd_ref, g_ref, b_ref, o_ref, acc_ref):
    ex = pl.program_id(1)

    @pl.when(ex == 0)
    def _():
        acc_ref[...] = jnp.zeros_like(acc_ref)

    xb = x_ref[...].astype(BF16)
    a = jnp.dot(xb, wg_ref[...], preferred_element_type=F32)
    u = jnp.dot(xb, wu_ref[...], preferred_element_type=F32)
    gates = gates_ref[...]
    lane = lax.broadcasted_iota(jnp.int32, gates.shape, 1)
    gate = jnp.sum(jnp.where(lane == ex, gates, 0.0), axis=-1, keepdims=True)
    hidden = (a * jax.nn.sigmoid(a)) * u * gate
    acc_ref[...] += jnp.dot(hidden.astype(BF16), wd_ref[...], preferred_element_type=F32)

    @pl.when(ex == N_EXPERTS - 1)
    def _():
        o_ref[...] = _layer_norm(alpha * x_ref[...] + acc_ref[...], g_ref[...], b_ref[...])


def _moe_ln(x2d, gates, w_gate, w_up, w_down, g, b, l, alpha):
    T, D = x2d.shape
    tm = 512
    return pl.pallas_call(
        functools.partial(_moe_kernel, alpha),
        grid=(T // tm, N_EXPERTS),
        in_specs=[pl.BlockSpec((tm, D), lambda i, e: (i, 0)),
                  pl.BlockSpec((tm, N_EXPERTS), lambda i, e: (i, 0)),
                  pl.BlockSpec((None, None, D, D_EXPERT), lambda i, e: (l, e, 0, 0)),
                  pl.BlockSpec((None, None, D, D_EXPERT), lambda i, e: (l, e, 0, 0)),
                  pl.BlockSpec((None, None, D_EXPERT, D), lambda i, e: (l, e, 0, 0)),
                  pl.BlockSpec((None, 1, D), lambda i, e: (l, 0, 0)),
                  pl.BlockSpec((None, 1, D), lambda i, e: (l, 0, 0))],
        out_specs=pl.BlockSpec((tm, D), lambda i, e: (i, 0)),
        out_shape=jax.ShapeDtypeStruct((T, D), F32),
        scratch_shapes=[pltpu.VMEM((tm, D), F32)],
        compiler_params=_params(("parallel", "arbitrary"), 48),
        name="moe_ln",
    )(x2d, gates, w_gate, w_up, w_down, g, b)


def kernel(x, positions, w_in, attn_sinks, conv_w, a_log, dt_bias, dn_norm_w, w_out, ln1_g, ln1_b,
           router_w, router_bias, w_gate, w_up, w_down, ln2_g, ln2_b):
    batch, seq, d = x.shape
    depth = w_in.shape[0]
    T = batch * seq
    alpha = float((2 * depth) ** 0.25)

    w_main = w_in[:, :, :MAIN_COLS].astype(BF16)
    w_tail = jnp.pad(w_in[:, :, MAIN_COLS:], ((0, 0), (0, 0), (0, LANE - 2 * DN_HEADS)))
    w_out_b = w_out.astype(BF16)
    w_gate_b, w_up_b, w_down_b = w_gate.astype(BF16), w_up.astype(BF16), w_down.astype(BF16)
    pad_heads = lambda t, n: jnp.pad(t, ((0, 0), (0, n - DN_HEADS)))
    alog_r = pad_heads(a_log, LANE).reshape(depth, 1, LANE)
    dtb_r = pad_heads(dt_bias, LANE).reshape(depth, 1, LANE)
    alog_c = pad_heads(a_log, 16).reshape(depth, 16, 1)
    dtb_c = pad_heads(dt_bias, 16).reshape(depth, 16, 1)
    masks = _delta_masks()
    ln1_g3, ln1_b3 = ln1_g.reshape(depth, 1, d), ln1_b.reshape(depth, 1, d)
    ln2_g3, ln2_b3 = ln2_g.reshape(depth, 1, d), ln2_b.reshape(depth, 1, d)

    cos, sin = _rope_tables(positions)
    h = x.reshape(T, d)
    for l in range(depth):
        proj = _inproj(h, w_main, l)
        tail = _tailproj(h, w_tail, l)
        tail_t = tail[:, :2 * DN_HEADS].T
        attn = _swa(proj, cos, sin, attn_sinks[l], batch, seq)
        dn = _gdn(proj, tail, tail_t, conv_w[l], alog_r[l], dtb_r[l], alog_c[l], dtb_c[l],
                  dn_norm_w[l].reshape(1, DN_HEAD_DIM), masks, batch, seq)
        h = _outproj_ln(attn, dn, w_out_b, h, ln1_g3, ln1_b3, l, alpha)
        gates = _router(h, router_w, router_bias)
        h = _moe_ln(h, gates, w_gate_b, w_up_b, w_down_b, ln2_g3, ln2_b3, l, alpha)
    return h.reshape(batch, seq, d)
```

```python
import functools

import numpy as np
import jax
import jax.numpy as jnp
from jax import lax
from jax.experimental import pallas as pl
from jax.experimental.pallas import tpu as pltpu

F32 = jnp.float32
BF16 = jnp.bfloat16
HIGHEST = lax.Precision.HIGHEST

D_MODEL = 2048
HEAD_DIM = 64
N_Q_HEADS = 16
KV_GROUP = 8
N_KV_HEADS = 2
ATT_BLOCK = 128
ROPE_THETA = 10000.0
DN_HEADS = 8
DN_HEAD_DIM = 128
CONV_WIDTH = 4
CHUNK = 64
SUPER = 256
N_EXPERTS = 16
N_GROUPS = 4
EXPERTS_PER_GROUP = 4
D_EXPERT = 512
LN_EPS = 1e-5
RMS_EPS = 1e-6
NEG = -1e30

LANE = 128
DNQ_BLK, DNK_BLK, DNV_BLK, Z_BLK = 0, 8, 16, 24
Q_BLK, K_BLK, V_BLK = 32, 40, 41
SWA_COLS = 10 * LANE
MAIN_COLS = 42 * LANE
GDN_HPB = 8
MIB = 1024 * 1024


def _params(sem, vmem_mib):
    return pltpu.CompilerParams(dimension_semantics=sem, vmem_limit_bytes=vmem_mib * MIB)


def _inproj_kernel(x_ref, w_ref, o_ref):
    o_ref[...] = jnp.dot(x_ref[...].astype(BF16), w_ref[...], preferred_element_type=F32)


def _inproj(x2d, w_main, l):
    T, K = x2d.shape
    tm, tn = 512, 768
    return pl.pallas_call(
        _inproj_kernel,
        grid=(T // tm, MAIN_COLS // tn),
        in_specs=[pl.BlockSpec((tm, K), lambda i, j: (i, 0)),
                  pl.BlockSpec((None, K, tn), lambda i, j: (l, 0, j))],
        out_specs=pl.BlockSpec((tm, tn), lambda i, j: (i, j)),
        out_shape=jax.ShapeDtypeStruct((T, MAIN_COLS), F32),
        compiler_params=_params(("parallel", "arbitrary"), 40),
        name="inproj",
    )(x2d, w_main)


def _tail_kernel(x_ref, w_ref, o_ref):
    o_ref[...] = jnp.dot(x_ref[...], w_ref[...], preferred_element_type=F32, precision=HIGHEST)


def _tailproj(x2d, w_tail, l):
    T, K = x2d.shape
    tm = 512
    return pl.pallas_call(
        _tail_kernel,
        grid=(T // tm,),
        in_specs=[pl.BlockSpec((tm, K), lambda i: (i, 0)),
                  pl.BlockSpec((None, K, LANE), lambda i: (l, 0, 0))],
        out_specs=pl.BlockSpec((tm, LANE), lambda i: (i, 0)),
        out_shape=jax.ShapeDtypeStruct((T, LANE), F32),
        compiler_params=_params(("parallel",), 32),
        name="tailproj",
    )(x2d, w_tail)


def _rope_kernel(pos_ref, inv_ref, cos_ref, sin_ref):
    ang = pos_ref[...].astype(F32) * inv_ref[...]
    lane = lax.broadcasted_iota(jnp.int32, ang.shape, 1)
    sign = jnp.where((lane & (HEAD_DIM - 1)) < HEAD_DIM // 2, -1.0, 1.0)
    cos_ref[...] = jnp.cos(ang)
    sin_ref[...] = jnp.sin(ang) * sign


def _rope_tables(positions):
    T = positions.size
    tm = min(1024, T)
    half = HEAD_DIM // 2
    inv_freq = 1.0 / (ROPE_THETA ** (jnp.arange(0, HEAD_DIM, 2, dtype=F32) / HEAD_DIM))
    inv = jnp.tile(inv_freq, LANE // half).reshape(1, LANE)
    return pl.pallas_call(
        _rope_kernel,
        grid=(T // tm,),
        in_specs=[pl.BlockSpec((tm, 1), lambda i: (i, 0)),
                  pl.BlockSpec((1, LANE), lambda i: (0, 0))],
        out_specs=[pl.BlockSpec((tm, LANE), lambda i: (i, 0))] * 2,
        out_shape=[jax.ShapeDtypeStruct((T, LANE), F32)] * 2,
        compiler_params=_params(("parallel",), 32),
        name="rope_tables",
    )(positions.reshape(T, 1), inv)


def _rope(x, cos, sin):
    lane = lax.broadcasted_iota(jnp.int32, x.shape, 1)
    first_half = (lane & (HEAD_DIM - 1)) < HEAD_DIM // 2
    swapped = jnp.where(first_half,
                        pltpu.roll(x, LANE - HEAD_DIM // 2, axis=1),
                        pltpu.roll(x, HEAD_DIM // 2, axis=1))
    return x * cos + swapped * sin


def _swa_kernel(sink_ref, q_ref, kc_ref, kp_ref, vc_ref, vp_ref, cc_ref, sc_ref, cp_ref, sp_ref, o_ref):
    blk = pl.program_id(1)
    cos_c, sin_c = cc_ref[...], sc_ref[...]
    k_cur = _rope(kc_ref[...], cos_c, sin_c)
    k_prev = _rope(kp_ref[...], cp_ref[...], sp_ref[...])
    k_cat = jnp.concatenate([k_prev, k_cur], axis=0).astype(BF16)
    v_cat = jnp.concatenate([vp_ref[...], vc_ref[...]], axis=0).astype(BF16)

    qi = lax.broadcasted_iota(jnp.int32, (ATT_BLOCK, 2 * ATT_BLOCK), 0)
    kj = lax.broadcasted_iota(jnp.int32, (ATT_BLOCK, 2 * ATT_BLOCK), 1)
    first_key = jnp.where(blk > 0, 0, ATT_BLOCK)
    valid = (kj > qi) & (kj <= qi + ATT_BLOCK) & (kj >= first_key)

    outs = []
    for pair in range(N_Q_HEADS // 2):
        qp = _rope(q_ref[:, pair * LANE:(pair + 1) * LANE], cos_c, sin_c) * (HEAD_DIM ** -0.5)
        qp = qp.astype(BF16)
        for sub in range(2):
            hd = 2 * pair + sub
            kvh = hd // KV_GROUP
            q_h = qp[:, sub * HEAD_DIM:(sub + 1) * HEAD_DIM]
            k_h = k_cat[:, kvh * HEAD_DIM:(kvh + 1) * HEAD_DIM]
            v_h = v_cat[:, kvh * HEAD_DIM:(kvh + 1) * HEAD_DIM]
            s = lax.dot_general(q_h, k_h, (((1,), (1,)), ((), ())), preferred_element_type=F32)
            s = jnp.where(valid, s, NEG)
            sink = sink_ref[hd]
            m = jnp.maximum(jnp.max(s, axis=-1, keepdims=True), sink)
            p = jnp.exp(s - m)
            denom = jnp.sum(p, axis=-1, keepdims=True) + jnp.exp(sink - m)
            o = jnp.dot(p.astype(BF16), v_h, preferred_element_type=F32)
            outs.append(o / denom)
    o_ref[...] = jnp.concatenate(outs, axis=1).astype(o_ref.dtype)


def _swa(proj, cos, sin, sinks, batch, seq):
    T = proj.shape[0]
    nb = seq // ATT_BLOCK
    cur = lambda b, i: b * nb + i
    prev = lambda b, i: b * nb + jnp.maximum(i - 1, 0)
    blk = lambda rows_fn, col: pl.BlockSpec((ATT_BLOCK, LANE), lambda b, i: (rows_fn(b, i), col))
    return pl.pallas_call(
        _swa_kernel,
        grid=(batch, nb),
        in_specs=[pl.BlockSpec(memory_space=pltpu.SMEM),
                  pl.BlockSpec((ATT_BLOCK, N_Q_HEADS * HEAD_DIM), lambda b, i: (cur(b, i), Q_BLK // (N_Q_HEADS * HEAD_DIM // LANE))),
                  blk(cur, K_BLK), blk(prev, K_BLK), blk(cur, V_BLK), blk(prev, V_BLK),
                  blk(cur, 0), blk(cur, 0), blk(prev, 0), blk(prev, 0)],
        out_specs=pl.BlockSpec((ATT_BLOCK, N_Q_HEADS * HEAD_DIM), lambda b, i: (cur(b, i), 0)),
        out_shape=jax.ShapeDtypeStruct((T, N_Q_HEADS * HEAD_DIM), BF16),
        compiler_params=_params(("parallel", "arbitrary"), 32),
        name="swa",
    )(sinks, proj, proj, proj, proj, proj, cos, sin, cos, sin)


def _delta_masks():
    r = np.arange(SUPER)[:, None]
    c = np.arange(SUPER)[None, :]
    same = (r // CHUNK) == (c // CHUNK)
    ms = [same & (r > c), same & (r >= c), ((r // 4) == (c // 4)) & (r > c)]
    s = 4
    while s < CHUNK:
        ms.append(((r // (2 * s)) == (c // (2 * s))) & ((r // s) % 2 == 1) & ((c // s) % 2 == 0))
        s *= 2
    return jnp.asarray(np.stack(ms).astype(np.float32))


def _softplus(x):
    return jnp.maximum(x, 0.0) + jnp.log(1.0 + jnp.exp(-jnp.abs(x)))


def _bdot(a, b):
    return jnp.dot(a.astype(BF16), b.astype(BF16), preferred_element_type=F32)


def _gdn_kernel(q_ref, k_ref, v_ref, hq_ref, hk_ref, hv_ref, cwq_ref, cwk_ref, cwv_ref,
                z_ref, tail_ref, tailt_ref, alog_r_ref, dtb_r_ref, alog_c_ref, dtb_c_ref,
                nw_ref, mask_ref, o_ref, state_ref):
    head0 = pl.program_id(1) * GDN_HPB
    step = pl.program_id(2)

    @pl.when(step == 0)
    def _():
        state_ref[...] = jnp.zeros_like(state_ref)

    def conv_silu(x_ref, halo_ref, w_ref, sl):
        halo = jnp.where(step > 0, halo_ref[:, sl], 0.0)
        xx = jnp.concatenate([halo, x_ref[:, sl]], axis=0)
        w = w_ref[:, sl]
        y = xx[8:8 + SUPER] * w[3:4]
        for j in range(CONV_WIDTH - 1):
            off = 8 - (CONV_WIDTH - 1) + j
            y = y + xx[off:off + SUPER] * w[j:j + 1]
        return y * jax.nn.sigmoid(y)

    def l2norm(x):
        return x * lax.rsqrt(jnp.sum(x * x, axis=-1, keepdims=True) + RMS_EPS)

    m_strict, m_causal = mask_ref[0], mask_ref[1]
    tail = tail_ref[...]
    lane = lax.broadcasted_iota(jnp.int32, tail.shape, 1)
    g_cols = -jnp.exp(alog_r_ref[...]) * _softplus(tail + dtb_r_ref[...])
    gc_cols = jnp.dot(m_causal, g_cols, preferred_element_type=F32, precision=HIGHEST)
    tail_t = tailt_ref[...]
    sub = lax.broadcasted_iota(jnp.int32, tail_t.shape, 0)
    g_rows = -jnp.exp(alog_c_ref[...]) * _softplus(tail_t + dtb_c_ref[...])
    gc_rows = lax.dot_general(g_rows, m_causal, (((1,), (1,)), ((), ())),
                              preferred_element_type=F32, precision=HIGHEST)

    ri = lax.broadcasted_iota(jnp.int32, (SUPER, SUPER), 0)
    ci = lax.broadcasted_iota(jnp.int32, (SUPER, SUPER), 1)
    eye = jnp.where(ri == ci, 1.0, 0.0)
    lane_r = lax.broadcasted_iota(jnp.int32, (1, SUPER), 1)

    heads = range(GDN_HPB)
    sls = [slice(j * LANE, (j + 1) * LANE) for j in heads]
    q = [l2norm(conv_silu(q_ref, hq_ref, cwq_ref, sl)) * (DN_HEAD_DIM ** -0.5) for sl in sls]
    k = [l2norm(conv_silu(k_ref, hk_ref, cwk_ref, sl)) for sl in sls]
    v = [conv_silu(v_ref, hv_ref, cwv_ref, sl) for sl in sls]
    gc_col = [jnp.sum(jnp.where(lane == head0 + j, gc_cols, 0.0), axis=-1, keepdims=True) for j in heads]
    beta = [jax.nn.sigmoid(jnp.sum(jnp.where(lane == DN_HEADS + head0 + j, tail, 0.0), axis=-1, keepdims=True))
            for j in heads]
    gc_row = [jnp.sum(jnp.where(sub == head0 + j, gc_rows, 0.0), axis=0, keepdims=True) for j in heads]
    decay = [jnp.exp(jnp.where(m_causal > 0, gc_col[j] - gc_row[j], NEG)) for j in heads]
    kb = [k[j] * beta[j] for j in heads]
    vb = [v[j] * beta[j] for j in heads]
    kq = [lax.dot_general(jnp.concatenate([kb[j], q[j]], axis=0).astype(BF16), k[j].astype(BF16),
                          (((1,), (1,)), ((), ())), preferred_element_type=F32) for j in heads]
    lmat = [kq[j][:SUPER] * decay[j] * m_strict for j in heads]
    a_qk = [kq[j][SUPER:] * decay[j] for j in heads]

    l4 = [lmat[j] * mask_ref[2] for j in heads]
    l4sq = [_bdot(l4[j], l4[j]) for j in heads]
    l4cu = [_bdot(l4[j], l4sq[j]) for j in heads]
    tinv = [eye - l4[j] + l4sq[j] - l4cu[j] for j in heads]
    for lvl in range(4):
        ct = [_bdot(lmat[j] * mask_ref[3 + lvl], tinv[j]) for j in heads]
        tinv = [tinv[j] - _bdot(tinv[j], ct[j]) for j in heads]

    e_col = [jnp.exp(gc_col[j]) for j in heads]
    uw = [_bdot(tinv[j], jnp.concatenate([vb[j], kb[j] * e_col[j]], axis=1)) for j in heads]
    qg = [q[j] * e_col[j] for j in heads]
    kd_t = []
    for j in heads:
        gc_last_row = jnp.zeros((1, SUPER), F32)
        for c in range(SUPER // CHUNK):
            last = gc_row[j][:, (c + 1) * CHUNK - 1:(c + 1) * CHUNK]
            gc_last_row = jnp.where((lane_r >> 6) == c, last, gc_last_row)
        kd_t.append(k[j].T * jnp.exp(gc_last_row - gc_row[j]))

    state = [state_ref[j] for j in heads]
    outs = [[] for _ in heads]
    for c in range(SUPER // CHUNK):
        r0 = c * CHUNK
        rows = slice(r0, r0 + CHUNK)
        ws = [_bdot(jnp.concatenate([uw[j][rows, LANE:], qg[j][rows]], axis=0), state[j]) for j in heads]
        v_new = [uw[j][rows, :LANE] - ws[j][:CHUNK] for j in heads]
        for j in heads:
            outs[j].append(ws[j][CHUNK:] + _bdot(a_qk[j][rows, rows], v_new[j]))
        state = [state[j] * jnp.exp(gc_row[j][:, r0 + CHUNK - 1:r0 + CHUNK]) + _bdot(kd_t[j][:, rows], v_new[j])
                 for j in heads]
    for j in heads:
        state_ref[j] = state[j]
        o = jnp.concatenate(outs[j], axis=0)
        z = z_ref[:, sls[j]]
        o = o * lax.rsqrt(jnp.mean(o * o, axis=-1, keepdims=True) + RMS_EPS) * nw_ref[...] * (z * jax.nn.sigmoid(z))
        o_ref[:, sls[j]] = o.astype(o_ref.dtype)


def _gdn(proj, tail, tail_t, conv_w, alog_r, dtb_r, alog_c, dtb_c, norm_w, masks, batch, seq):
    T = proj.shape[0]
    ns = seq // SUPER
    hpb = GDN_HPB
    halo_per_super = SUPER // 8
    row = lambda b, i: b * ns + i
    halo_row = lambda b, i: jnp.maximum((b * ns + i) * halo_per_super - 1, 0)
    cur = lambda col: pl.BlockSpec((SUPER, hpb * LANE), lambda b, h, i: (row(b, i), col // hpb + h))
    halo = lambda col: pl.BlockSpec((8, hpb * LANE), lambda b, h, i: (halo_row(b, i), col // hpb + h))
    cw = lambda col: pl.BlockSpec((CONV_WIDTH, hpb * LANE), lambda b, h, i: (0, col // hpb + h))
    const2 = lambda shape: pl.BlockSpec(shape, lambda b, h, i: (0, 0))
    return pl.pallas_call(
        _gdn_kernel,
        grid=(batch, DN_HEADS // hpb, ns),
        in_specs=[cur(DNQ_BLK), cur(DNK_BLK), cur(DNV_BLK),
                  halo(DNQ_BLK), halo(DNK_BLK), halo(DNV_BLK),
                  cw(0), cw(DN_HEADS), cw(2 * DN_HEADS),
                  cur(Z_BLK),
                  pl.BlockSpec((SUPER, LANE), lambda b, h, i: (row(b, i), 0)),
                  pl.BlockSpec((16, SUPER), lambda b, h, i: (0, row(b, i))),
                  const2((1, LANE)), const2((1, LANE)), const2((16, 1)), const2((16, 1)),
                  const2((1, LANE)),
                  pl.BlockSpec(masks.shape, lambda b, h, i: (0, 0, 0))],
        out_specs=pl.BlockSpec((SUPER, hpb * LANE), lambda b, h, i: (row(b, i), h)),
        out_shape=jax.ShapeDtypeStruct((T, DN_HEADS * DN_HEAD_DIM), BF16),
        scratch_shapes=[pltpu.VMEM((hpb, DN_HEAD_DIM, DN_HEAD_DIM), F32)],
        compiler_params=_params(("parallel", "parallel", "arbitrary"), 40),
        name="gdn",
    )(proj, proj, proj, proj, proj, proj, conv_w, conv_w, conv_w, proj, tail, tail_t,
      alog_r, dtb_r, alog_c, dtb_c, norm_w, masks)


def _layer_norm(y, g, b):
    mu = jnp.mean(y, axis=-1, keepdims=True)
    yc = y - mu
    var = jnp.mean(yc * yc, axis=-1, keepdims=True)
    return yc * lax.rsqrt(var + LN_EPS) * g + b


def _outproj_kernel(alpha, a_ref, d_ref, wa_ref, wd_ref, x_ref, g_ref, b_ref, o_ref):
    mix = jnp.dot(a_ref[...], wa_ref[...], preferred_element_type=F32)
    mix = mix + jnp.dot(d_ref[...], wd_ref[...], preferred_element_type=F32)
    o_ref[...] = _layer_norm(alpha * x_ref[...] + mix, g_ref[...], b_ref[...])


def _outproj_ln(attn, dn, w_out, x2d, g, b, l, alpha):
    T, D = x2d.shape
    tm = 256
    half = attn.shape[1]
    return pl.pallas_call(
        functools.partial(_outproj_kernel, alpha),
        grid=(T // tm,),
        in_specs=[pl.BlockSpec((tm, half), lambda i: (i, 0)),
                  pl.BlockSpec((tm, half), lambda i: (i, 0)),
                  pl.BlockSpec((None, half, D), lambda i: (l, 0, 0)),
                  pl.BlockSpec((None, half, D), lambda i: (l, 1, 0)),
                  pl.BlockSpec((tm, D), lambda i: (i, 0)),
                  pl.BlockSpec((None, 1, D), lambda i: (l, 0, 0)),
                  pl.BlockSpec((None, 1, D), lambda i: (l, 0, 0))],
        out_specs=pl.BlockSpec((tm, D), lambda i: (i, 0)),
        out_shape=jax.ShapeDtypeStruct((T, D), F32),
        compiler_params=_params(("parallel",), 48),
        name="outproj_ln",
    )(attn, dn, w_out, w_out, x2d, g, b)


def _router_kernel(x_ref, w_ref, bias_ref, gates_ref):
    logits = jnp.dot(x_ref[...], w_ref[...], preferred_element_type=F32, precision=HIGHEST)
    scores = jax.nn.sigmoid(logits)
    sel = scores + bias_ref[...]
    e = lax.broadcasted_iota(jnp.int32, sel.shape, 1)
    big = jnp.int32(N_EXPERTS)

    def top2(vals):
        m1 = jnp.max(vals, axis=-1, keepdims=True)
        i1 = jnp.min(jnp.where(vals == m1, e, big), axis=-1, keepdims=True)
        rest = jnp.where(e == i1, -jnp.inf, vals)
        m2 = jnp.max(rest, axis=-1, keepdims=True)
        i2 = jnp.min(jnp.where(rest == m2, e, big), axis=-1, keepdims=True)
        return m1, i1, m2, i2

    best = None
    for grp in range(N_GROUPS):
        m1, i1, m2, i2 = top2(jnp.where((e >> 2) == grp, sel, -jnp.inf))
        gsum = m1 + m2
        if best is None:
            best = (gsum, i1, i2)
        else:
            take = gsum > best[0]
            best = (jnp.where(take, gsum, best[0]), jnp.where(take, i1, best[1]), jnp.where(take, i2, best[2]))
    _, e1, e2 = best
    w1 = jnp.sum(jnp.where(e == e1, scores, 0.0), axis=-1, keepdims=True)
    w2 = jnp.sum(jnp.where(e == e2, scores, 0.0), axis=-1, keepdims=True)
    tot = w1 + w2
    gates_ref[...] = jnp.where(e == e1, w1 / tot, 0.0) + jnp.where(e == e2, w2 / tot, 0.0)


def _router(x2d, router_w, router_bias):
    T, D = x2d.shape
    tm = 512
    return pl.pallas_call(
        _router_kernel,
        grid=(T // tm,),
        in_specs=[pl.BlockSpec((tm, D), lambda i: (i, 0)),
                  pl.BlockSpec((D, N_EXPERTS), lambda i: (0, 0)),
                  pl.BlockSpec((1, N_EXPERTS), lambda i: (0, 0))],
        out_specs=pl.BlockSpec((tm, N_EXPERTS), lambda i: (i, 0)),
        out_shape=jax.ShapeDtypeStruct((T, N_EXPERTS), F32),
        compiler_params=_params(("parallel",), 32),
        name="router",
    )(x2d, router_w, router_bias.reshape(1, N_EXPERTS))


def _moe_kernel(alpha, x_ref, gates_ref, wg_ref, wu_ref, wd_ref, g_ref, b_ref, o_ref, acc_ref):
    ex = pl.program_id(1)

    @pl.when(ex == 0)
    def _():
        acc_ref[...] = jnp.zeros_like(acc_ref)

    xb = x_ref[...].astype(BF16)
    a = jnp.dot(xb, wg_ref[...], preferred_element_type=F32)
    u = jnp.dot(xb, wu_ref[...], preferred_element_type=F32)
    gates = gates_ref[...]
    lane = lax.broadcasted_iota(jnp.int32, gates.shape, 1)
    gate = jnp.sum(jnp.where(lane == ex, gates, 0.0), axis=-1, keepdims=True)
    hidden = (a * jax.nn.sigmoid(a)) * u * gate
    acc_ref[...] += jnp.dot(hidden.astype(BF16), wd_ref[...], preferred_element_type=F32)

    @pl.when(ex == N_EXPERTS - 1)
    def _():
        o_ref[...] = _layer_norm(alpha * x_ref[...] + acc_ref[...], g_ref[...], b_ref[...])


def _moe_ln(x2d, gates, w_gate, w_up, w_down, g, b, l, alpha):
    T, D = x2d.shape
    tm = 512
    return pl.pallas_call(
        functools.partial(_moe_kernel, alpha),
        grid=(T // tm, N_EXPERTS),
        in_specs=[pl.BlockSpec((tm, D), lambda i, e: (i, 0)),
                  pl.BlockSpec((tm, N_EXPERTS), lambda i, e: (i, 0)),
                  pl.BlockSpec((None, None, D, D_EXPERT), lambda i, e: (l, e, 0, 0)),
                  pl.BlockSpec((None, None, D, D_EXPERT), lambda i, e: (l, e, 0, 0)),
                  pl.BlockSpec((None, None, D_EXPERT, D), lambda i, e: (l, e, 0, 0)),
                  pl.BlockSpec((None, 1, D), lambda i, e: (l, 0, 0)),
                  pl.BlockSpec((None, 1, D), lambda i, e: (l, 0, 0))],
        out_specs=pl.BlockSpec((tm, D), lambda i, e: (i, 0)),
        out_shape=jax.ShapeDtypeStruct((T, D), F32),
        scratch_shapes=[pltpu.VMEM((tm, D), F32)],
        compiler_params=_params(("parallel", "arbitrary"), 48),
        name="moe_ln",
    )(x2d, gates, w_gate, w_up, w_down, g, b)


def kernel(x, positions, w_in, attn_sinks, conv_w, a_log, dt_bias, dn_norm_w, w_out, ln1_g, ln1_b,
           router_w, router_bias, w_gate, w_up, w_down, ln2_g, ln2_b):
    batch, seq, d = x.shape
    depth = w_in.shape[0]
    T = batch * seq
    alpha = float((2 * depth) ** 0.25)

    w_main = jnp.concatenate([w_in[:, :, SWA_COLS:MAIN_COLS], w_in[:, :, :SWA_COLS]], axis=-1).astype(BF16)
    w_tail = jnp.pad(w_in[:, :, MAIN_COLS:], ((0, 0), (0, 0), (0, LANE - 2 * DN_HEADS)))
    w_out_b = w_out.astype(BF16)
    w_gate_b, w_up_b, w_down_b = w_gate.astype(BF16), w_up.astype(BF16), w_down.astype(BF16)
    pad_heads = lambda t, n: jnp.pad(t, ((0, 0), (0, n - DN_HEADS)))
    alog_r = pad_heads(a_log, LANE).reshape(depth, 1, LANE)
    dtb_r = pad_heads(dt_bias, LANE).reshape(depth, 1, LANE)
    alog_c = pad_heads(a_log, 16).reshape(depth, 16, 1)
    dtb_c = pad_heads(dt_bias, 16).reshape(depth, 16, 1)
    masks = _delta_masks()
    ln1_g3, ln1_b3 = ln1_g.reshape(depth, 1, d), ln1_b.reshape(depth, 1, d)
    ln2_g3, ln2_b3 = ln2_g.reshape(depth, 1, d), ln2_b.reshape(depth, 1, d)

    cos, sin = _rope_tables(positions)
    h = x.reshape(T, d)
    for l in range(depth):
        proj = _inproj(h, w_main, l)
        tail = _tailproj(h, w_tail, l)
        tail_t = tail[:, :2 * DN_HEADS].T
        attn = _swa(proj, cos, sin, attn_sinks[l], batch, seq)
        dn = _gdn(proj, tail, tail_t, conv_w[l], alog_r[l], dtb_r[l], alog_c[l], dtb_c[l],
                  dn_norm_w[l].reshape(1, DN_HEAD_DIM), masks, batch, seq)
        h = _outproj_ln(attn, dn, w_out_b, h, ln1_g3, ln1_b3, l, alpha)
        gates = _router(h, router_w, router_bias)
        h = _moe_ln(h, gates, w_gate_b, w_up_b, w_down_b, ln2_g3, ln2_b3, l, alpha)
    return h.reshape(batch, seq, d)
```

```python
import functools

import numpy as np
import jax
import jax.numpy as jnp
from jax import lax
from jax.experimental import pallas as pl
from jax.experimental.pallas import tpu as pltpu

F32 = jnp.float32
BF16 = jnp.bfloat16
HIGHEST = lax.Precision.HIGHEST

D_MODEL = 2048
HEAD_DIM = 64
N_Q_HEADS = 16
KV_GROUP = 8
N_KV_HEADS = 2
ATT_BLOCK = 128
ROPE_THETA = 10000.0
DN_HEADS = 8
DN_HEAD_DIM = 128
CONV_WIDTH = 4
CHUNK = 64
SUPER = 256
N_EXPERTS = 16
N_GROUPS = 4
EXPERTS_PER_GROUP = 4
D_EXPERT = 512
LN_EPS = 1e-5
RMS_EPS = 1e-6
NEG = -1e30

LANE = 128
DNQ_BLK, DNK_BLK, DNV_BLK, Z_BLK = 0, 8, 16, 24
Q_BLK, K_BLK, V_BLK = 32, 40, 41
SWA_COLS = 10 * LANE
MAIN_COLS = 42 * LANE
GDN_HPB = 8
MIB = 1024 * 1024


def _params(sem, vmem_mib):
    return pltpu.CompilerParams(dimension_semantics=sem, vmem_limit_bytes=vmem_mib * MIB)


def _split_bf16(t):
    hi = t.astype(BF16)
    return hi, (t - hi.astype(F32)).astype(BF16)


def _inproj_kernel(x_ref, w_ref, wt_ref, o_ref, t_ref, xh_ref):
    @pl.when(pl.program_id(1) == 0)
    def _():
        xh, xl = _split_bf16(x_ref[...])
        xh_ref[...] = xh
        wt = wt_ref[...]
        both = jnp.dot(xh, wt, preferred_element_type=F32) + jnp.dot(xl, wt, preferred_element_type=F32)
        t_ref[...] = both[:, :LANE] + both[:, LANE:]

    o_ref[...] = jnp.dot(xh_ref[...], w_ref[...], preferred_element_type=F32)


def _inproj(x2d, w_main, w_tail, l):
    T, K = x2d.shape
    tm, tn = 512, 768
    return pl.pallas_call(
        _inproj_kernel,
        grid=(T // tm, MAIN_COLS // tn),
        in_specs=[pl.BlockSpec((tm, K), lambda i, j: (i, 0)),
                  pl.BlockSpec((None, K, tn), lambda i, j: (l, 0, j)),
                  pl.BlockSpec((None, K, 2 * LANE), lambda i, j: (l, 0, 0))],
        out_specs=[pl.BlockSpec((tm, tn), lambda i, j: (i, j)),
                   pl.BlockSpec((tm, LANE), lambda i, j: (i, 0))],
        out_shape=[jax.ShapeDtypeStruct((T, MAIN_COLS), F32),
                   jax.ShapeDtypeStruct((T, LANE), F32)],
        scratch_shapes=[pltpu.VMEM((tm, K), BF16)],
        compiler_params=_params(("parallel", "arbitrary"), 40),
        name="inproj",
    )(x2d, w_main, w_tail)


def _rope_kernel(pos_ref, inv_ref, cos_ref, sin_ref):
    ang = pos_ref[...].astype(F32) * inv_ref[...]
    lane = lax.broadcasted_iota(jnp.int32, ang.shape, 1)
    sign = jnp.where((lane & (HEAD_DIM - 1)) < HEAD_DIM // 2, -1.0, 1.0)
    cos_ref[...] = jnp.cos(ang)
    sin_ref[...] = jnp.sin(ang) * sign


def _rope_tables(positions):
    T = positions.size
    tm = min(1024, T)
    half = HEAD_DIM // 2
    inv_freq = 1.0 / (ROPE_THETA ** (jnp.arange(0, HEAD_DIM, 2, dtype=F32) / HEAD_DIM))
    inv = jnp.tile(inv_freq, LANE // half).reshape(1, LANE)
    return pl.pallas_call(
        _rope_kernel,
        grid=(T // tm,),
        in_specs=[pl.BlockSpec((tm, 1), lambda i: (i, 0)),
                  pl.BlockSpec((1, LANE), lambda i: (0, 0))],
        out_specs=[pl.BlockSpec((tm, LANE), lambda i: (i, 0))] * 2,
        out_shape=[jax.ShapeDtypeStruct((T, LANE), F32)] * 2,
        compiler_params=_params(("parallel",), 32),
        name="rope_tables",
    )(positions.reshape(T, 1), inv)


def _rope(x, cos, sin):
    lane = lax.broadcasted_iota(jnp.int32, x.shape, 1)
    first_half = (lane & (HEAD_DIM - 1)) < HEAD_DIM // 2
    swapped = jnp.where(first_half,
                        pltpu.roll(x, LANE - HEAD_DIM // 2, axis=1),
                        pltpu.roll(x, HEAD_DIM // 2, axis=1))
    return x * cos + swapped * sin


def _swa_kernel(sink_ref, q_ref, kc_ref, kp_ref, vc_ref, vp_ref, cc_ref, sc_ref, cp_ref, sp_ref, o_ref):
    blk = pl.program_id(1)
    cos_c, sin_c = cc_ref[...], sc_ref[...]
    k_cur = _rope(kc_ref[...], cos_c, sin_c)
    k_prev = _rope(kp_ref[...], cp_ref[...], sp_ref[...])
    k_cat = jnp.concatenate([k_prev, k_cur], axis=0).astype(BF16)
    v_cat = jnp.concatenate([vp_ref[...], vc_ref[...]], axis=0).astype(BF16)

    qi = lax.broadcasted_iota(jnp.int32, (ATT_BLOCK, 2 * ATT_BLOCK), 0)
    kj = lax.broadcasted_iota(jnp.int32, (ATT_BLOCK, 2 * ATT_BLOCK), 1)
    first_key = jnp.where(blk > 0, 0, ATT_BLOCK)
    valid = (kj > qi) & (kj <= qi + ATT_BLOCK) & (kj >= first_key)

    lane = lax.broadcasted_iota(jnp.int32, (ATT_BLOCK, LANE), 1)
    low_head = lane < HEAD_DIM
    pairs_per_kv = KV_GROUP // 2
    group = range(KV_GROUP)
    for kvh in range(N_KV_HEADS):
        k_h = k_cat[:, kvh * HEAD_DIM:(kvh + 1) * HEAD_DIM]
        v_h = v_cat[:, kvh * HEAD_DIM:(kvh + 1) * HEAD_DIM]
        k_dup = jnp.concatenate([k_h, k_h], axis=1)
        v_dup = jnp.concatenate([v_h, v_h], axis=1)
        q_rows = []
        for pp in range(pairs_per_kv):
            pair = kvh * pairs_per_kv + pp
            qp = _rope(q_ref[:, pair * LANE:(pair + 1) * LANE], cos_c, sin_c) * (HEAD_DIM ** -0.5)
            q_rows += [jnp.where(low_head, qp, 0.0), jnp.where(low_head, 0.0, qp)]
        q_stack = jnp.concatenate(q_rows, axis=0).astype(BF16)
        s_all = lax.dot_general(q_stack, k_dup, (((1,), (1,)), ((), ())), preferred_element_type=F32)
        sinks = [sink_ref[kvh * KV_GROUP + g] for g in group]
        s = [jnp.where(valid, s_all[g * ATT_BLOCK:(g + 1) * ATT_BLOCK], NEG) for g in group]
        m = [jnp.maximum(jnp.max(s[g], axis=-1, keepdims=True), sinks[g]) for g in group]
        p = [jnp.exp(s[g] - m[g]) for g in group]
        inv = [1.0 / (jnp.sum(p[g], axis=-1, keepdims=True) + jnp.exp(sinks[g] - m[g])) for g in group]
        p_stack = jnp.concatenate([p[g].astype(BF16) for g in group], axis=0)
        o_all = jnp.dot(p_stack, v_dup, preferred_element_type=F32)
        for pp in range(pairs_per_kv):
            pair = kvh * pairs_per_kv + pp
            ge, go = 2 * pp, 2 * pp + 1
            o_even = o_all[ge * ATT_BLOCK:(ge + 1) * ATT_BLOCK] * inv[ge]
            o_odd = o_all[go * ATT_BLOCK:(go + 1) * ATT_BLOCK] * inv[go]
            o_ref[:, pair * LANE:(pair + 1) * LANE] = jnp.where(low_head, o_even, o_odd).astype(o_ref.dtype)


def _swa(proj, cos, sin, sinks, batch, seq):
    T = proj.shape[0]
    nb = seq // ATT_BLOCK
    cur = lambda b, i: b * nb + i
    prev = lambda b, i: b * nb + jnp.maximum(i - 1, 0)
    blk = lambda rows_fn, col: pl.BlockSpec((ATT_BLOCK, LANE), lambda b, i: (rows_fn(b, i), col))
    return pl.pallas_call(
        _swa_kernel,
        grid=(batch, nb),
        in_specs=[pl.BlockSpec(memory_space=pltpu.SMEM),
                  pl.BlockSpec((ATT_BLOCK, N_Q_HEADS * HEAD_DIM), lambda b, i: (cur(b, i), Q_BLK // (N_Q_HEADS * HEAD_DIM // LANE))),
                  blk(cur, K_BLK), blk(prev, K_BLK), blk(cur, V_BLK), blk(prev, V_BLK),
                  blk(cur, 0), blk(cur, 0), blk(prev, 0), blk(prev, 0)],
        out_specs=pl.BlockSpec((ATT_BLOCK, N_Q_HEADS * HEAD_DIM), lambda b, i: (cur(b, i), 0)),
        out_shape=jax.ShapeDtypeStruct((T, N_Q_HEADS * HEAD_DIM), BF16),
        compiler_params=_params(("parallel", "arbitrary"), 32),
        name="swa",
    )(sinks, proj, proj, proj, proj, proj, cos, sin, cos, sin)


def _delta_masks():
    r = np.arange(SUPER)[:, None]
    c = np.arange(SUPER)[None, :]
    same = (r // CHUNK) == (c // CHUNK)
    ms = [same & (r > c), same & (r >= c), ((r // 4) == (c // 4)) & (r > c)]
    s = 4
    while s < CHUNK:
        ms.append(((r // (2 * s)) == (c // (2 * s))) & ((r // s) % 2 == 1) & ((c // s) % 2 == 0))
        s *= 2
    return jnp.asarray(np.stack(ms).astype(np.float32))


def _softplus(x):
    return jnp.maximum(x, 0.0) + jnp.log(1.0 + jnp.exp(-jnp.abs(x)))


def _bdot(a, b):
    return jnp.dot(a.astype(BF16), b.astype(BF16), preferred_element_type=F32)


def _gdn_kernel(q_ref, k_ref, v_ref, hq_ref, hk_ref, hv_ref, cwq_ref, cwk_ref, cwv_ref,
                z_ref, tail_ref, tailt_ref, alog_r_ref, dtb_r_ref, alog_c_ref, dtb_c_ref,
                nw_ref, mask_ref, o_ref, state_ref):
    head0 = pl.program_id(1) * GDN_HPB
    step = pl.program_id(2)

    @pl.when(step == 0)
    def _():
        state_ref[...] = jnp.zeros_like(state_ref)

    def conv_silu(x_ref, halo_ref, w_ref, sl):
        halo = jnp.where(step > 0, halo_ref[:, sl], 0.0)
        xx = jnp.concatenate([halo, x_ref[:, sl]], axis=0)
        w = w_ref[:, sl]
        y = xx[8:8 + SUPER] * w[3:4]
        for j in range(CONV_WIDTH - 1):
            off = 8 - (CONV_WIDTH - 1) + j
            y = y + xx[off:off + SUPER] * w[j:j + 1]
        return y * jax.nn.sigmoid(y)

    def l2norm(x):
        return x * lax.rsqrt(jnp.sum(x * x, axis=-1, keepdims=True) + RMS_EPS)

    m_strict, m_causal = mask_ref[0], mask_ref[1]
    tail = tail_ref[...]
    lane = lax.broadcasted_iota(jnp.int32, tail.shape, 1)
    g_cols = -jnp.exp(alog_r_ref[...]) * _softplus(tail + dtb_r_ref[...])
    gc_cols = jnp.dot(m_causal, g_cols, preferred_element_type=F32, precision=HIGHEST)
    tail_t = tailt_ref[...]
    sub = lax.broadcasted_iota(jnp.int32, tail_t.shape, 0)
    g_rows = -jnp.exp(alog_c_ref[...]) * _softplus(tail_t + dtb_c_ref[...])
    gc_rows = lax.dot_general(g_rows, m_causal, (((1,), (1,)), ((), ())),
                              preferred_element_type=F32, precision=HIGHEST)

    ri = lax.broadcasted_iota(jnp.int32, (SUPER, SUPER), 0)
    ci = lax.broadcasted_iota(jnp.int32, (SUPER, SUPER), 1)
    eye = jnp.where(ri == ci, 1.0, 0.0)
    lane_r = lax.broadcasted_iota(jnp.int32, (1, SUPER), 1)

    heads = range(GDN_HPB)
    sls = [slice(j * LANE, (j + 1) * LANE) for j in heads]
    q = [l2norm(conv_silu(q_ref, hq_ref, cwq_ref, sl)) * (DN_HEAD_DIM ** -0.5) for sl in sls]
    k = [l2norm(conv_silu(k_ref, hk_ref, cwk_ref, sl)) for sl in sls]
    v = [conv_silu(v_ref, hv_ref, cwv_ref, sl) for sl in sls]
    gc_col = [jnp.sum(jnp.where(lane == head0 + j, gc_cols, 0.0), axis=-1, keepdims=True) for j in heads]
    beta = [jax.nn.sigmoid(jnp.sum(jnp.where(lane == DN_HEADS + head0 + j, tail, 0.0), axis=-1, keepdims=True))
            for j in heads]
    gc_row = [jnp.sum(jnp.where(sub == head0 + j, gc_rows, 0.0), axis=0, keepdims=True) for j in heads]
    decay = [jnp.exp(jnp.where(m_causal > 0, gc_col[j] - gc_row[j], NEG)) for j in heads]
    kb = [k[j] * beta[j] for j in heads]
    vb = [v[j] * beta[j] for j in heads]
    kq = [lax.dot_general(jnp.concatenate([kb[j], q[j]], axis=0).astype(BF16), k[j].astype(BF16),
                          (((1,), (1,)), ((), ())), preferred_element_type=F32) for j in heads]
    lmat = [kq[j][:SUPER] * decay[j] * m_strict for j in heads]
    a_qk = [kq[j][SUPER:] * decay[j] for j in heads]

    l4 = [lmat[j] * mask_ref[2] for j in heads]
    l4sq = [_bdot(l4[j], l4[j]) for j in heads]
    l4cu = [_bdot(l4[j], l4sq[j]) for j in heads]
    tinv = [eye - l4[j] + l4sq[j] - l4cu[j] for j in heads]
    for lvl in range(4):
        ct = [_bdot(lmat[j] * mask_ref[3 + lvl], tinv[j]) for j in heads]
        tinv = [tinv[j] - _bdot(tinv[j], ct[j]) for j in heads]

    e_col = [jnp.exp(gc_col[j]) for j in heads]
    uw = [_bdot(tinv[j], jnp.concatenate([vb[j], kb[j] * e_col[j]], axis=1)) for j in heads]
    qg = [q[j] * e_col[j] for j in heads]
    kd_t = []
    for j in heads:
        gc_last_row = jnp.zeros((1, SUPER), F32)
        for c in range(SUPER // CHUNK):
            last = gc_row[j][:, (c + 1) * CHUNK - 1:(c + 1) * CHUNK]
            gc_last_row = jnp.where((lane_r >> 6) == c, last, gc_last_row)
        kd_t.append(k[j].T * jnp.exp(gc_last_row - gc_row[j]))

    state = [state_ref[j] for j in heads]
    outs = [[] for _ in heads]
    for c in range(SUPER // CHUNK):
        r0 = c * CHUNK
        rows = slice(r0, r0 + CHUNK)
        ws = [_bdot(jnp.concatenate([uw[j][rows, LANE:], qg[j][rows]], axis=0), state[j]) for j in heads]
        v_new = [uw[j][rows, :LANE] - ws[j][:CHUNK] for j in heads]
        for j in heads:
            outs[j].append(ws[j][CHUNK:] + _bdot(a_qk[j][rows, rows], v_new[j]))
        state = [state[j] * jnp.exp(gc_row[j][:, r0 + CHUNK - 1:r0 + CHUNK]) + _bdot(kd_t[j][:, rows], v_new[j])
                 for j in heads]
    for j in heads:
        state_ref[j] = state[j]
        o = jnp.concatenate(outs[j], axis=0)
        z = z_ref[:, sls[j]]
        o = o * lax.rsqrt(jnp.mean(o * o, axis=-1, keepdims=True) + RMS_EPS) * nw_ref[...] * (z * jax.nn.sigmoid(z))
        o_ref[:, sls[j]] = o.astype(o_ref.dtype)


def _gdn(proj, tail, tail_t, conv_w, alog_r, dtb_r, alog_c, dtb_c, norm_w, masks, batch, seq):
    T = proj.shape[0]
    ns = seq // SUPER
    hpb = GDN_HPB
    halo_per_super = SUPER // 8
    row = lambda b, i: b * ns + i
    halo_row = lambda b, i: jnp.maximum((b * ns + i) * halo_per_super - 1, 0)
    cur = lambda col: pl.BlockSpec((SUPER, hpb * LANE), lambda b, h, i: (row(b, i), col // hpb + h))
    halo = lambda col: pl.BlockSpec((8, hpb * LANE), lambda b, h, i: (halo_row(b, i), col // hpb + h))
    cw = lambda col: pl.BlockSpec((CONV_WIDTH, hpb * LANE), lambda b, h, i: (0, col // hpb + h))
    const2 = lambda shape: pl.BlockSpec(shape, lambda b, h, i: (0, 0))
    return pl.pallas_call(
        _gdn_kernel,
        grid=(batch, DN_HEADS // hpb, ns),
        in_specs=[cur(DNQ_BLK), cur(DNK_BLK), cur(DNV_BLK),
                  halo(DNQ_BLK), halo(DNK_BLK), halo(DNV_BLK),
                  cw(0), cw(DN_HEADS), cw(2 * DN_HEADS),
                  cur(Z_BLK),
                  pl.BlockSpec((SUPER, LANE), lambda b, h, i: (row(b, i), 0)),
                  pl.BlockSpec((16, SUPER), lambda b, h, i: (0, row(b, i))),
                  const2((1, LANE)), const2((1, LANE)), const2((16, 1)), const2((16, 1)),
                  const2((1, LANE)),
                  pl.BlockSpec(masks.shape, lambda b, h, i: (0, 0, 0))],
        out_specs=pl.BlockSpec((SUPER, hpb * LANE), lambda b, h, i: (row(b, i), h)),
        out_shape=jax.ShapeDtypeStruct((T, DN_HEADS * DN_HEAD_DIM), BF16),
        scratch_shapes=[pltpu.VMEM((hpb, DN_HEAD_DIM, DN_HEAD_DIM), F32)],
        compiler_params=_params(("parallel", "parallel", "arbitrary"), 40),
        name="gdn",
    )(proj, proj, proj, proj, proj, proj, conv_w, conv_w, conv_w, proj, tail, tail_t,
      alog_r, dtb_r, alog_c, dtb_c, norm_w, masks)


def _layer_norm(y, g, b):
    mu = jnp.mean(y, axis=-1, keepdims=True)
    yc = y - mu
    var = jnp.mean(yc * yc, axis=-1, keepdims=True)
    return yc * lax.rsqrt(var + LN_EPS) * g + b


def _outproj_kernel(alpha, a_ref, d_ref, wa_ref, wd_ref, x_ref, g_ref, b_ref, o_ref):
    mix = jnp.dot(a_ref[...], wa_ref[...], preferred_element_type=F32)
    mix = mix + jnp.dot(d_ref[...], wd_ref[...], preferred_element_type=F32)
    o_ref[...] = _layer_norm(alpha * x_ref[...] + mix, g_ref[...], b_ref[...])


def _outproj_ln(attn, dn, w_out, x2d, g, b, l, alpha):
    T, D = x2d.shape
    tm = 256
    half = attn.shape[1]
    return pl.pallas_call(
        functools.partial(_outproj_kernel, alpha),
        grid=(T // tm,),
        in_specs=[pl.BlockSpec((tm, half), lambda i: (i, 0)),
                  pl.BlockSpec((tm, half), lambda i: (i, 0)),
                  pl.BlockSpec((None, half, D), lambda i: (l, 0, 0)),
                  pl.BlockSpec((None, half, D), lambda i: (l, 1, 0)),
                  pl.BlockSpec((tm, D), lambda i: (i, 0)),
                  pl.BlockSpec((None, 1, D), lambda i: (l, 0, 0)),
                  pl.BlockSpec((None, 1, D), lambda i: (l, 0, 0))],
        out_specs=pl.BlockSpec((tm, D), lambda i: (i, 0)),
        out_shape=jax.ShapeDtypeStruct((T, D), F32),
        compiler_params=_params(("parallel",), 48),
        name="outproj_ln",
    )(attn, dn, w_out, w_out, x2d, g, b)


def _router_kernel(x_ref, w_ref, bias_ref, gates_ref):
    xh, xl = _split_bf16(x_ref[...])
    w_parts = jnp.concatenate(_split_bf16(w_ref[...]), axis=1)
    both = jnp.dot(xh, w_parts, preferred_element_type=F32) + jnp.dot(xl, w_parts, preferred_element_type=F32)
    logits = both[:, :N_EXPERTS] + both[:, N_EXPERTS:]
    scores = jax.nn.sigmoid(logits)
    sel = scores + bias_ref[...]
    e = lax.broadcasted_iota(jnp.int32, sel.shape, 1)
    big = jnp.int32(N_EXPERTS)

    def top2(vals):
        m1 = jnp.max(vals, axis=-1, keepdims=True)
        i1 = jnp.min(jnp.where(vals == m1, e, big), axis=-1, keepdims=True)
        rest = jnp.where(e == i1, -jnp.inf, vals)
        m2 = jnp.max(rest, axis=-1, keepdims=True)
        i2 = jnp.min(jnp.where(rest == m2, e, big), axis=-1, keepdims=True)
        return m1, i1, m2, i2

    best = None
    for grp in range(N_GROUPS):
        m1, i1, m2, i2 = top2(jnp.where((e >> 2) == grp, sel, -jnp.inf))
        gsum = m1 + m2
        if best is None:
            best = (gsum, i1, i2)
        else:
            take = gsum > best[0]
            best = (jnp.where(take, gsum, best[0]), jnp.where(take, i1, best[1]), jnp.where(take, i2, best[2]))
    _, e1, e2 = best
    w1 = jnp.sum(jnp.where(e == e1, scores, 0.0), axis=-1, keepdims=True)
    w2 = jnp.sum(jnp.where(e == e2, scores, 0.0), axis=-1, keepdims=True)
    tot = w1 + w2
    gates_ref[...] = jnp.where(e == e1, w1 / tot, 0.0) + jnp.where(e == e2, w2 / tot, 0.0)


def _router(x2d, router_w, router_bias):
    T, D = x2d.shape
    tm = 512
    return pl.pallas_call(
        _router_kernel,
        grid=(T // tm,),
        in_specs=[pl.BlockSpec((tm, D), lambda i: (i, 0)),
                  pl.BlockSpec((D, N_EXPERTS), lambda i: (0, 0)),
                  pl.BlockSpec((1, N_EXPERTS), lambda i: (0, 0))],
        out_specs=pl.BlockSpec((tm, N_EXPERTS), lambda i: (i, 0)),
        out_shape=jax.ShapeDtypeStruct((T, N_EXPERTS), F32),
        compiler_params=_params(("parallel",), 32),
        name="router",
    )(x2d, router_w, router_bias.reshape(1, N_EXPERTS))


def _moe_kernel(alpha, x_ref, gates_ref, wg_ref, wu_ref, wd_ref, g_ref, b_ref, o_ref, acc_ref, xb_ref):
    ex = pl.program_id(1)

    @pl.when(ex == 0)
    def _():
        acc_ref[...] = jnp.zeros_like(acc_ref)
        xb_ref[...] = x_ref[...].astype(BF16)

    xb = xb_ref[...]
    a = jnp.dot(xb, wg_ref[...], preferred_element_type=F32)
    u = jnp.dot(xb, wu_ref[...], preferred_element_type=F32)
    gates = gates_ref[...]
    lane = lax.broadcasted_iota(jnp.int32, gates.shape, 1)
    gate = jnp.sum(jnp.where(lane == ex, gates, 0.0), axis=-1, keepdims=True)
    hidden = (a * jax.nn.sigmoid(a)) * u * gate
    acc_ref[...] += jnp.dot(hidden.astype(BF16), wd_ref[...], preferred_element_type=F32)

    @pl.when(ex == N_EXPERTS - 1)
    def _():
        o_ref[...] = _layer_norm(alpha * x_ref[...] + acc_ref[...], g_ref[...], b_ref[...])


def _moe_ln(x2d, gates, w_gate, w_up, w_down, g, b, l, alpha):
    T, D = x2d.shape
    tm = 512
    return pl.pallas_call(
        functools.partial(_moe_kernel, alpha),
        grid=(T // tm, N_EXPERTS),
        in_specs=[pl.BlockSpec((tm, D), lambda i, e: (i, 0)),
                  pl.BlockSpec((tm, N_EXPERTS), lambda i, e: (i, 0)),
                  pl.BlockSpec((None, None, D, D_EXPERT), lambda i, e: (l, e, 0, 0)),
                  pl.BlockSpec((None, None, D, D_EXPERT), lambda i, e: (l, e, 0, 0)),
                  pl.BlockSpec((None, None, D_EXPERT, D), lambda i, e: (l, e, 0, 0)),
                  pl.BlockSpec((None, 1, D), lambda i, e: (l, 0, 0)),
                  pl.BlockSpec((None, 1, D), lambda i, e: (l, 0, 0))],
        out_specs=pl.BlockSpec((tm, D), lambda i, e: (i, 0)),
        out_shape=jax.ShapeDtypeStruct((T, D), F32),
        scratch_shapes=[pltpu.VMEM((tm, D), F32), pltpu.VMEM((tm, D), BF16)],
        compiler_params=_params(("parallel", "arbitrary"), 48),
        name="moe_ln",
    )(x2d, gates, w_gate, w_up, w_down, g, b)


def kernel(x, positions, w_in, attn_sinks, conv_w, a_log, dt_bias, dn_norm_w, w_out, ln1_g, ln1_b,
           router_w, router_bias, w_gate, w_up, w_down, ln2_g, ln2_b):
    batch, seq, d = x.shape
    depth = w_in.shape[0]
    T = batch * seq
    alpha = float((2 * depth) ** 0.25)

    w_main = jnp.concatenate([w_in[:, :, SWA_COLS:MAIN_COLS].astype(BF16), w_in[:, :, :SWA_COLS].astype(BF16)],
                             axis=-1)
    w_tail = jnp.concatenate(_split_bf16(
        jnp.pad(w_in[:, :, MAIN_COLS:], ((0, 0), (0, 0), (0, LANE - 2 * DN_HEADS)))), axis=-1)
    w_out_b = w_out.astype(BF16)
    w_gate_b, w_up_b, w_down_b = w_gate.astype(BF16), w_up.astype(BF16), w_down.astype(BF16)
    pad_heads = lambda t, n: jnp.pad(t, ((0, 0), (0, n - DN_HEADS)))
    alog_r = pad_heads(a_log, LANE).reshape(depth, 1, LANE)
    dtb_r = pad_heads(dt_bias, LANE).reshape(depth, 1, LANE)
    alog_c = pad_heads(a_log, 16).reshape(depth, 16, 1)
    dtb_c = pad_heads(dt_bias, 16).reshape(depth, 16, 1)
    masks = _delta_masks()
    ln1_g3, ln1_b3 = ln1_g.reshape(depth, 1, d), ln1_b.reshape(depth, 1, d)
    ln2_g3, ln2_b3 = ln2_g.reshape(depth, 1, d), ln2_b.reshape(depth, 1, d)

    cos, sin = _rope_tables(positions)
    h = x.reshape(T, d)
    for l in range(depth):
        proj, tail = _inproj(h, w_main, w_tail, l)
        tail_t = tail[:, :2 * DN_HEADS].T
        attn = _swa(proj, cos, sin, attn_sinks[l], batch, seq)
        dn = _gdn(proj, tail, tail_t, conv_w[l], alog_r[l], dtb_r[l], alog_c[l], dtb_c[l],
                  dn_norm_w[l].reshape(1, DN_HEAD_DIM), masks, batch, seq)
        h = _outproj_ln(attn, dn, w_out_b, h, ln1_g3, ln1_b3, l, alpha)
        gates = _router(h, router_w, router_bias)
        h = _moe_ln(h, gates, w_gate_b, w_up_b, w_down_b, ln2_g3, ln2_b3, l, alpha)
    return h.reshape(batch, seq, d)
```

```python
import functools

import numpy as np
import jax
import jax.numpy as jnp
from jax import lax
from jax.experimental import pallas as pl
from jax.experimental.pallas import tpu as pltpu

F32 = jnp.float32
BF16 = jnp.bfloat16
HIGHEST = lax.Precision.HIGHEST

D_MODEL = 2048
HEAD_DIM = 64
N_Q_HEADS = 16
KV_GROUP = 8
N_KV_HEADS = 2
ATT_BLOCK = 128
ROPE_THETA = 10000.0
DN_HEADS = 8
DN_HEAD_DIM = 128
CONV_WIDTH = 4
CHUNK = 64
SUPER = 256
N_EXPERTS = 16
N_GROUPS = 4
EXPERTS_PER_GROUP = 4
D_EXPERT = 512
LN_EPS = 1e-5
RMS_EPS = 1e-6
NEG = -1e30

LANE = 128
Q_BLK, K_BLK, V_BLK = 0, 8, 9
DNQ_BLK, DNK_BLK, DNV_BLK, Z_BLK = 10, 18, 26, 34
MAIN_COLS = 42 * LANE
GDN_HPB = 8
GDN_SPLIT = 2
MIB = 1024 * 1024


def _params(sem, vmem_mib):
    return pltpu.CompilerParams(dimension_semantics=sem, vmem_limit_bytes=vmem_mib * MIB)


def _split_bf16(t):
    hi = t.astype(BF16)
    return hi, (t - hi.astype(F32)).astype(BF16)


def _inproj_kernel(x_ref, w_ref, wt_ref, o_ref, t_ref, xh_ref):
    @pl.when(pl.program_id(1) == 0)
    def _():
        xh, xl = _split_bf16(x_ref[...])
        xh_ref[...] = xh
        wt = wt_ref[...]
        both = jnp.dot(xh, wt, preferred_element_type=F32) + jnp.dot(xl, wt, preferred_element_type=F32)
        t_ref[...] = both[:, :LANE] + both[:, LANE:]

    o_ref[...] = jnp.dot(xh_ref[...], w_ref[...], preferred_element_type=F32)


def _inproj(x2d, w_main, w_tail, l):
    T, K = x2d.shape
    tm, tn = min(1024, T), 768
    return pl.pallas_call(
        _inproj_kernel,
        grid=(T // tm, MAIN_COLS // tn),
        in_specs=[pl.BlockSpec((tm, K), lambda i, j: (i, 0)),
                  pl.BlockSpec((None, K, tn), lambda i, j: (l, 0, j)),
                  pl.BlockSpec((None, K, 2 * LANE), lambda i, j: (l, 0, 0))],
        out_specs=[pl.BlockSpec((tm, tn), lambda i, j: (i, j)),
                   pl.BlockSpec((tm, LANE), lambda i, j: (i, 0))],
        out_shape=[jax.ShapeDtypeStruct((T, MAIN_COLS), F32),
                   jax.ShapeDtypeStruct((T, LANE), F32)],
        scratch_shapes=[pltpu.VMEM((tm, K), BF16)],
        compiler_params=_params(("parallel", "arbitrary"), 40),
        name="inproj",
    )(x2d, w_main, w_tail)


def _rope_kernel(pos_ref, inv_ref, cos_ref, sin_ref):
    ang = pos_ref[...].astype(F32) * inv_ref[...]
    lane = lax.broadcasted_iota(jnp.int32, ang.shape, 1)
    sign = jnp.where((lane & (HEAD_DIM - 1)) < HEAD_DIM // 2, -1.0, 1.0)
    cos_ref[...] = jnp.cos(ang)
    sin_ref[...] = jnp.sin(ang) * sign


def _rope_tables(positions):
    T = positions.size
    tm = min(1024, T)
    half = HEAD_DIM // 2
    inv_freq = 1.0 / (ROPE_THETA ** (jnp.arange(0, HEAD_DIM, 2, dtype=F32) / HEAD_DIM))
    inv = jnp.tile(inv_freq, LANE // half).reshape(1, LANE)
    return pl.pallas_call(
        _rope_kernel,
        grid=(T // tm,),
        in_specs=[pl.BlockSpec((tm, 1), lambda i: (i, 0)),
                  pl.BlockSpec((1, LANE), lambda i: (0, 0))],
        out_specs=[pl.BlockSpec((tm, LANE), lambda i: (i, 0))] * 2,
        out_shape=[jax.ShapeDtypeStruct((T, LANE), F32)] * 2,
        compiler_params=_params(("parallel",), 32),
        name="rope_tables",
    )(positions.reshape(T, 1), inv)


def _rope(x, cos, sin):
    lane = lax.broadcasted_iota(jnp.int32, x.shape, 1)
    first_half = (lane & (HEAD_DIM - 1)) < HEAD_DIM // 2
    swapped = jnp.where(first_half,
                        pltpu.roll(x, LANE - HEAD_DIM // 2, axis=1),
                        pltpu.roll(x, HEAD_DIM // 2, axis=1))
    return x * cos + swapped * sin


def _swa_kernel(sink_ref, q_ref, kc_ref, kp_ref, vc_ref, vp_ref, cc_ref, sc_ref, cp_ref, sp_ref, o_ref):
    blk = pl.program_id(1)
    cos_c, sin_c = cc_ref[...], sc_ref[...]
    k_cur = _rope(kc_ref[...], cos_c, sin_c)
    k_prev = _rope(kp_ref[...], cp_ref[...], sp_ref[...])
    k_cat = jnp.concatenate([k_prev, k_cur], axis=0).astype(BF16)
    v_cat = jnp.concatenate([vp_ref[...], vc_ref[...]], axis=0).astype(BF16)

    qi = lax.broadcasted_iota(jnp.int32, (ATT_BLOCK, 2 * ATT_BLOCK), 0)
    kj = lax.broadcasted_iota(jnp.int32, (ATT_BLOCK, 2 * ATT_BLOCK), 1)
    first_key = jnp.where(blk > 0, 0, ATT_BLOCK)
    valid = (kj > qi) & (kj <= qi + ATT_BLOCK) & (kj >= first_key)

    lane = lax.broadcasted_iota(jnp.int32, (ATT_BLOCK, LANE), 1)
    low_head = lane < HEAD_DIM
    pairs_per_kv = KV_GROUP // 2
    group = range(KV_GROUP)
    for kvh in range(N_KV_HEADS):
        k_h = k_cat[:, kvh * HEAD_DIM:(kvh + 1) * HEAD_DIM]
        v_h = v_cat[:, kvh * HEAD_DIM:(kvh + 1) * HEAD_DIM]
        k_dup = jnp.concatenate([k_h, k_h], axis=1)
        v_dup = jnp.concatenate([v_h, v_h], axis=1)
        q_rows = []
        for pp in range(pairs_per_kv):
            pair = kvh * pairs_per_kv + pp
            qp = _rope(q_ref[:, pair * LANE:(pair + 1) * LANE], cos_c, sin_c) * (HEAD_DIM ** -0.5)
            q_rows += [jnp.where(low_head, qp, 0.0), jnp.where(low_head, 0.0, qp)]
        q_stack = jnp.concatenate(q_rows, axis=0).astype(BF16)
        s_all = lax.dot_general(q_stack, k_dup, (((1,), (1,)), ((), ())), preferred_element_type=F32)
        sinks = [sink_ref[kvh * KV_GROUP + g] for g in group]
        s = [jnp.where(valid, s_all[g * ATT_BLOCK:(g + 1) * ATT_BLOCK], NEG) for g in group]
        m = [jnp.maximum(jnp.max(s[g], axis=-1, keepdims=True), sinks[g]) for g in group]
        p = [jnp.exp(s[g] - m[g]) for g in group]
        inv = [1.0 / (jnp.sum(p[g], axis=-1, keepdims=True) + jnp.exp(sinks[g] - m[g])) for g in group]
        p_stack = jnp.concatenate([p[g].astype(BF16) for g in group], axis=0)
        o_all = jnp.dot(p_stack, v_dup, preferred_element_type=F32)
        for pp in range(pairs_per_kv):
            pair = kvh * pairs_per_kv + pp
            ge, go = 2 * pp, 2 * pp + 1
            o_even = o_all[ge * ATT_BLOCK:(ge + 1) * ATT_BLOCK] * inv[ge]
            o_odd = o_all[go * ATT_BLOCK:(go + 1) * ATT_BLOCK] * inv[go]
            o_ref[:, pair * LANE:(pair + 1) * LANE] = jnp.where(low_head, o_even, o_odd).astype(o_ref.dtype)


def _swa(proj, cos, sin, sinks, batch, seq):
    T = proj.shape[0]
    nb = seq // ATT_BLOCK
    cur = lambda b, i: b * nb + i
    prev = lambda b, i: b * nb + jnp.maximum(i - 1, 0)
    blk = lambda rows_fn, col: pl.BlockSpec((ATT_BLOCK, LANE), lambda b, i: (rows_fn(b, i), col))
    return pl.pallas_call(
        _swa_kernel,
        grid=(batch, nb),
        in_specs=[pl.BlockSpec(memory_space=pltpu.SMEM),
                  pl.BlockSpec((ATT_BLOCK, N_Q_HEADS * HEAD_DIM), lambda b, i: (cur(b, i), Q_BLK // (N_Q_HEADS * HEAD_DIM // LANE))),
                  blk(cur, K_BLK), blk(prev, K_BLK), blk(cur, V_BLK), blk(prev, V_BLK),
                  blk(cur, 0), blk(cur, 0), blk(prev, 0), blk(prev, 0)],
        out_specs=pl.BlockSpec((ATT_BLOCK, N_Q_HEADS * HEAD_DIM), lambda b, i: (cur(b, i), 0)),
        out_shape=jax.ShapeDtypeStruct((T, N_Q_HEADS * HEAD_DIM), BF16),
        compiler_params=_params(("parallel", "arbitrary"), 32),
        name="swa",
    )(sinks, proj, proj, proj, proj, proj, cos, sin, cos, sin)


def _delta_masks():
    i = np.arange(CHUNK)[:, None]
    j = (np.arange(SUPER) % CHUNK)[None, :]
    ms = [i > j, i >= j, ((i // 4) == (j // 4)) & (i > j)]
    s = 4
    while s < CHUNK:
        ms.append(((i // (2 * s)) == (j // (2 * s))) & ((i // s) % 2 == 1) & ((j // s) % 2 == 0))
        s *= 2
    ms.append(i == j)
    r = np.arange(SUPER)[:, None]
    c = np.arange(SUPER)[None, :]
    same = (r // CHUNK) == (c // CHUNK)
    return (jnp.asarray(np.stack(ms).astype(np.float32)),
            jnp.asarray((same & (r >= c)).astype(np.float32)),
            jnp.asarray(same.astype(np.float32), dtype=BF16))


def _softplus(x):
    return jnp.maximum(x, 0.0) + jnp.log(1.0 + jnp.exp(-jnp.abs(x)))


def _bdot(a, b):
    return jnp.dot(a.astype(BF16), b.astype(BF16), preferred_element_type=F32)


def _gdn_kernel(*refs):
    pieces = GDN_HPB // GDN_SPLIT
    q_refs, k_refs, v_refs, hq_refs, hk_refs, hv_refs, z_refs = (refs[i * pieces:(i + 1) * pieces] for i in range(7))
    (cwq_ref, cwk_ref, cwv_ref, tail_ref, tailt_ref, alog_r_ref, dtb_r_ref, alog_c_ref, dtb_c_ref,
     nw_ref, wmask_ref, causal_ref, same_ref, o_ref, state_ref) = refs[7 * pieces:]
    head0 = pl.program_id(1) * GDN_HPB
    step = pl.program_id(2)
    heads = range(GDN_HPB)
    sls = [slice(j * LANE, (j + 1) * LANE) for j in heads]

    @pl.when(step == 0)
    def _():
        state_ref[...] = jnp.zeros_like(state_ref)

    def head_block(piece_refs, j):
        return piece_refs[j // GDN_SPLIT][:, (j % GDN_SPLIT) * LANE:(j % GDN_SPLIT + 1) * LANE]

    def conv_silu(x_refs, halo_refs, w_ref, j):
        halo = jnp.where(step > 0, head_block(halo_refs, j), 0.0)
        xx = jnp.concatenate([halo, head_block(x_refs, j)], axis=0)
        w = w_ref[:, sls[j]]
        y = xx[8:] * w[CONV_WIDTH - 1:CONV_WIDTH]
        for j in range(CONV_WIDTH - 1):
            y = y + pltpu.roll(xx, CONV_WIDTH - 1 - j, axis=0)[8:] * w[j:j + 1]
        return y * jax.nn.sigmoid(y)

    def l2norm(x, scale):
        return x * (lax.rsqrt(jnp.sum(x * x, axis=-1, keepdims=True) + RMS_EPS) * scale)

    m_causal = causal_ref[...]
    tail = tail_ref[...]
    lane = lax.broadcasted_iota(jnp.int32, tail.shape, 1)
    g_cols = -jnp.exp(alog_r_ref[...]) * _softplus(tail + dtb_r_ref[...])
    gc_cols = jnp.dot(m_causal, g_cols, preferred_element_type=F32, precision=HIGHEST)
    tail_t = tailt_ref[...]
    sub = lax.broadcasted_iota(jnp.int32, tail_t.shape, 0)
    g_rows = -jnp.exp(alog_c_ref[...]) * _softplus(tail_t + dtb_c_ref[...])
    gc_rows = lax.dot_general(g_rows, m_causal, (((1,), (1,)), ((), ())),
                              preferred_element_type=F32, precision=HIGHEST)

    lane_r = lax.broadcasted_iota(jnp.int32, (1, SUPER), 1)
    n_chunks = SUPER // CHUNK
    chunk_of_lane = lax.broadcasted_iota(jnp.int32, (CHUNK, SUPER), 1) >> 6
    same_bd = same_ref[...]
    w_strict, w_causal, w_base, w_eye = wmask_ref[0], wmask_ref[1], wmask_ref[2], wmask_ref[3 + 4]

    def diag_blocks_wide(full):
        out = full[(n_chunks - 1) * CHUNK:]
        for c in range(n_chunks - 2, -1, -1):
            out = jnp.where(chunk_of_lane == c, full[c * CHUNK:(c + 1) * CHUNK], out)
        return out

    def column_wide(col):
        out = jnp.broadcast_to(col[(n_chunks - 1) * CHUNK:], (CHUNK, SUPER))
        for c in range(n_chunks - 2, -1, -1):
            out = jnp.where(chunk_of_lane == c, jnp.broadcast_to(col[c * CHUNK:(c + 1) * CHUNK], (CHUNK, SUPER)), out)
        return out

    def block_diag(wide):
        return jnp.concatenate([wide.astype(BF16)] * n_chunks, axis=0) * same_bd

    def wdot(a_wide, b_wide):
        return jnp.dot(a_wide.astype(BF16), block_diag(b_wide), preferred_element_type=F32)

    q = [l2norm(conv_silu(q_refs, hq_refs, cwq_ref, j), DN_HEAD_DIM ** -0.5) for j in heads]
    k = [l2norm(conv_silu(k_refs, hk_refs, cwk_ref, j), 1.0) for j in heads]
    v = [conv_silu(v_refs, hv_refs, cwv_ref, j) for j in heads]
    gc_col = [jnp.sum(jnp.where(lane == head0 + j, gc_cols, 0.0), axis=-1, keepdims=True) for j in heads]
    beta = [jax.nn.sigmoid(jnp.sum(jnp.where(lane == DN_HEADS + head0 + j, tail, 0.0), axis=-1, keepdims=True))
            for j in heads]
    gc_row = [jnp.sum(jnp.where(sub == head0 + j, gc_rows, 0.0), axis=0, keepdims=True) for j in heads]
    decay = [jnp.exp(jnp.where(w_causal > 0, column_wide(gc_col[j]) - gc_row[j], NEG)) for j in heads]
    kb = [k[j] * beta[j] for j in heads]
    vb = [v[j] * beta[j] for j in heads]
    kq = [lax.dot_general(jnp.concatenate([kb[j], q[j]], axis=0).astype(BF16), k[j].astype(BF16),
                          (((1,), (1,)), ((), ())), preferred_element_type=F32) for j in heads]
    lmat = [diag_blocks_wide(kq[j][:SUPER]) * decay[j] * w_strict for j in heads]
    a_qk = [diag_blocks_wide(kq[j][SUPER:]) * decay[j] for j in heads]

    l4 = [lmat[j] * w_base for j in heads]
    l4sq = [wdot(l4[j], l4[j]) for j in heads]
    l4cu = [wdot(l4[j], l4sq[j]) for j in heads]
    tinv = [w_eye - l4[j] + l4sq[j] - l4cu[j] for j in heads]
    for lvl in range(4):
        ct = [wdot(lmat[j] * wmask_ref[3 + lvl], tinv[j]) for j in heads]
        tinv = [tinv[j] - wdot(tinv[j], ct[j]) for j in heads]

    e_col = [jnp.exp(gc_col[j]) for j in heads]
    uw = [jnp.dot(block_diag(tinv[j]), jnp.concatenate([vb[j], kb[j] * e_col[j]], axis=1).astype(BF16),
                  preferred_element_type=F32) for j in heads]
    qg = [q[j] * e_col[j] for j in heads]
    kd_t = []
    for j in heads:
        gc_last_row = jnp.zeros((1, SUPER), F32)
        for c in range(SUPER // CHUNK):
            last = gc_row[j][:, (c + 1) * CHUNK - 1:(c + 1) * CHUNK]
            gc_last_row = jnp.where((lane_r >> 6) == c, last, gc_last_row)
        kd_t.append(k[j].T * jnp.exp(gc_last_row - gc_row[j]))

    ak = [jnp.concatenate([a_qk[j], kd_t[j]], axis=0).astype(BF16) for j in heads]
    state = [state_ref[j] for j in heads]
    outs = [[] for _ in heads]
    for c in range(n_chunks):
        r0 = c * CHUNK
        rows = slice(r0, r0 + CHUNK)
        ws = [_bdot(jnp.concatenate([uw[j][rows, LANE:], qg[j][rows]], axis=0), state[j]) for j in heads]
        v_new = [uw[j][rows, :LANE] - ws[j][:CHUNK] for j in heads]
        v_pad = [jnp.concatenate(([jnp.zeros((r0, LANE), BF16)] if r0 else []) + [v_new[j].astype(BF16)]
                                 + ([jnp.zeros((SUPER - r0 - CHUNK, LANE), BF16)] if r0 + CHUNK < SUPER else []),
                                 axis=0) for j in heads]
        both = [jnp.dot(ak[j], v_pad[j], preferred_element_type=F32) for j in heads]
        for j in heads:
            outs[j].append(ws[j][CHUNK:] + both[j][:CHUNK])
        state = [state[j] * jnp.exp(gc_row[j][:, r0 + CHUNK - 1:r0 + CHUNK]) + both[j][CHUNK:] for j in heads]
    for j in heads:
        state_ref[j] = state[j]
        o = jnp.concatenate(outs[j], axis=0)
        z = head_block(z_refs, j)
        o = o * lax.rsqrt(jnp.mean(o * o, axis=-1, keepdims=True) + RMS_EPS) * nw_ref[...] * (z * jax.nn.sigmoid(z))
        o_ref[:, sls[j]] = o.astype(o_ref.dtype)


def _gdn(proj, tail, tail_t, conv_w, alog_r, dtb_r, alog_c, dtb_c, norm_w, masks, batch, seq):
    T = proj.shape[0]
    ns = seq // SUPER
    hpb = GDN_HPB
    halo_per_super = SUPER // 8
    row = lambda b, i: b * ns + i
    halo_row = lambda b, i: jnp.maximum((b * ns + i) * halo_per_super - 1, 0)
    pieces = hpb // GDN_SPLIT
    piece_col = lambda col, h, m: col // GDN_SPLIT + h * pieces + m
    cur = lambda col: [pl.BlockSpec((SUPER, GDN_SPLIT * LANE), lambda b, h, i, m=m: (row(b, i), piece_col(col, h, m)))
                       for m in range(pieces)]
    halo = lambda col: [pl.BlockSpec((8, GDN_SPLIT * LANE), lambda b, h, i, m=m: (halo_row(b, i), piece_col(col, h, m)))
                        for m in range(pieces)]
    cw = lambda col: pl.BlockSpec((CONV_WIDTH, hpb * LANE), lambda b, h, i: (0, col // hpb + h))
    const2 = lambda shape: pl.BlockSpec(shape, lambda b, h, i: (0, 0))
    return pl.pallas_call(
        _gdn_kernel,
        grid=(batch, DN_HEADS // hpb, ns),
        in_specs=[*cur(DNQ_BLK), *cur(DNK_BLK), *cur(DNV_BLK),
                  *halo(DNQ_BLK), *halo(DNK_BLK), *halo(DNV_BLK), *cur(Z_BLK),
                  cw(0), cw(DN_HEADS), cw(2 * DN_HEADS),
                  pl.BlockSpec((SUPER, LANE), lambda b, h, i: (row(b, i), 0)),
                  pl.BlockSpec((16, SUPER), lambda b, h, i: (0, row(b, i))),
                  const2((1, LANE)), const2((1, LANE)), const2((16, 1)), const2((16, 1)),
                  const2((1, LANE)),
                  pl.BlockSpec(masks[0].shape, lambda b, h, i: (0, 0, 0)),
                  const2((SUPER, SUPER)), const2((SUPER, SUPER))],
        out_specs=pl.BlockSpec((SUPER, hpb * LANE), lambda b, h, i: (row(b, i), h)),
        out_shape=jax.ShapeDtypeStruct((T, DN_HEADS * DN_HEAD_DIM), BF16),
        scratch_shapes=[pltpu.VMEM((hpb, DN_HEAD_DIM, DN_HEAD_DIM), F32)],
        compiler_params=_params(("parallel", "parallel", "arbitrary"), 40),
        name="gdn",
    )(*([proj] * (7 * pieces)), conv_w, conv_w, conv_w, tail, tail_t,
      alog_r, dtb_r, alog_c, dtb_c, norm_w, *masks)


def _layer_norm(y, g, b):
    mu = jnp.mean(y, axis=-1, keepdims=True)
    yc = y - mu
    var = jnp.mean(yc * yc, axis=-1, keepdims=True)
    return yc * lax.rsqrt(var + LN_EPS) * g + b


def _outproj_kernel(alpha, a_ref, d_ref, wa_ref, wd_ref, x_ref, g_ref, b_ref, o_ref):
    mix = jnp.dot(a_ref[...], wa_ref[...], preferred_element_type=F32)
    mix = mix + jnp.dot(d_ref[...], wd_ref[...], preferred_element_type=F32)
    o_ref[...] = _layer_norm(alpha * x_ref[...] + mix, g_ref[...], b_ref[...])


def _outproj_ln(attn, dn, w_out, x2d, g, b, l, alpha):
    T, D = x2d.shape
    tm = 256
    half = attn.shape[1]
    return pl.pallas_call(
        functools.partial(_outproj_kernel, alpha),
        grid=(T // tm,),
        in_specs=[pl.BlockSpec((tm, half), lambda i: (i, 0)),
                  pl.BlockSpec((tm, half), lambda i: (i, 0)),
                  pl.BlockSpec((None, half, D), lambda i: (l, 0, 0)),
                  pl.BlockSpec((None, half, D), lambda i: (l, 1, 0)),
                  pl.BlockSpec((tm, D), lambda i: (i, 0)),
                  pl.BlockSpec((None, 1, D), lambda i: (l, 0, 0)),
                  pl.BlockSpec((None, 1, D), lambda i: (l, 0, 0))],
        out_specs=pl.BlockSpec((tm, D), lambda i: (i, 0)),
        out_shape=jax.ShapeDtypeStruct((T, D), F32),
        compiler_params=_params(("parallel",), 48),
        name="outproj_ln",
    )(attn, dn, w_out, w_out, x2d, g, b)


def _router_kernel(x_ref, w_ref, bias_ref, gates_ref):
    xh, xl = _split_bf16(x_ref[...])
    w_parts = jnp.concatenate(_split_bf16(w_ref[...]), axis=1)
    both = jnp.dot(xh, w_parts, preferred_element_type=F32) + jnp.dot(xl, w_parts, preferred_element_type=F32)
    logits = both[:, :N_EXPERTS] + both[:, N_EXPERTS:]
    scores = jax.nn.sigmoid(logits)
    sel = scores + bias_ref[...]
    e = lax.broadcasted_iota(jnp.int32, sel.shape, 1)
    big = jnp.int32(N_EXPERTS)

    def top2(vals):
        m1 = jnp.max(vals, axis=-1, keepdims=True)
        i1 = jnp.min(jnp.where(vals == m1, e, big), axis=-1, keepdims=True)
        rest = jnp.where(e == i1, -jnp.inf, vals)
        m2 = jnp.max(rest, axis=-1, keepdims=True)
        i2 = jnp.min(jnp.where(rest == m2, e, big), axis=-1, keepdims=True)
        return m1, i1, m2, i2

    best = None
    for grp in range(N_GROUPS):
        m1, i1, m2, i2 = top2(jnp.where((e >> 2) == grp, sel, -jnp.inf))
        gsum = m1 + m2
        if best is None:
            best = (gsum, i1, i2)
        else:
            take = gsum > best[0]
            best = (jnp.where(take, gsum, best[0]), jnp.where(take, i1, best[1]), jnp.where(take, i2, best[2]))
    _, e1, e2 = best
    w1 = jnp.sum(jnp.where(e == e1, scores, 0.0), axis=-1, keepdims=True)
    w2 = jnp.sum(jnp.where(e == e2, scores, 0.0), axis=-1, keepdims=True)
    tot = w1 + w2
    gates_ref[...] = jnp.where(e == e1, w1 / tot, 0.0) + jnp.where(e == e2, w2 / tot, 0.0)


def _router(x2d, router_w, router_bias):
    T, D = x2d.shape
    tm = 512
    return pl.pallas_call(
        _router_kernel,
        grid=(T // tm,),
        in_specs=[pl.BlockSpec((tm, D), lambda i: (i, 0)),
                  pl.BlockSpec((D, N_EXPERTS), lambda i: (0, 0)),
                  pl.BlockSpec((1, N_EXPERTS), lambda i: (0, 0))],
        out_specs=pl.BlockSpec((tm, N_EXPERTS), lambda i: (i, 0)),
        out_shape=jax.ShapeDtypeStruct((T, N_EXPERTS), F32),
        compiler_params=_params(("parallel",), 32),
        name="router",
    )(x2d, router_w, router_bias.reshape(1, N_EXPERTS))


def _moe_kernel(alpha, x_ref, gates_ref, wg_ref, wu_ref, wd_ref, g_ref, b_ref, o_ref, acc_ref, xb_ref):
    ex = pl.program_id(1)

    @pl.when(ex == 0)
    def _():
        acc_ref[...] = jnp.zeros_like(acc_ref)
        xb_ref[...] = x_ref[...].astype(BF16)

    xb = xb_ref[...]
    a = jnp.dot(xb, wg_ref[...], preferred_element_type=F32)
    u = jnp.dot(xb, wu_ref[...], preferred_element_type=F32)
    gates = gates_ref[...]
    lane = lax.broadcasted_iota(jnp.int32, gates.shape, 1)
    gate = jnp.sum(jnp.where(lane == ex, gates, 0.0), axis=-1, keepdims=True)
    hidden = (a * jax.nn.sigmoid(a)) * u * gate
    acc_ref[...] += jnp.dot(hidden.astype(BF16), wd_ref[...], preferred_element_type=F32)

    @pl.when(ex == N_EXPERTS - 1)
    def _():
        o_ref[...] = _layer_norm(alpha * x_ref[...] + acc_ref[...], g_ref[...], b_ref[...])


def _moe_ln(x2d, gates, w_gate, w_up, w_down, g, b, l, alpha):
    T, D = x2d.shape
    tm = 512
    return pl.pallas_call(
        functools.partial(_moe_kernel, alpha),
        grid=(T // tm, N_EXPERTS),
        in_specs=[pl.BlockSpec((tm, D), lambda i, e: (i, 0)),
                  pl.BlockSpec((tm, N_EXPERTS), lambda i, e: (i, 0)),
                  pl.BlockSpec((None, None, D, D_EXPERT), lambda i, e: (l, e, 0, 0)),
                  pl.BlockSpec((None, None, D, D_EXPERT), lambda i, e: (l, e, 0, 0)),
                  pl.BlockSpec((None, None, D_EXPERT, D), lambda i, e: (l, e, 0, 0)),
                  pl.BlockSpec((None, 1, D), lambda i, e: (l, 0, 0)),
                  pl.BlockSpec((None, 1, D), lambda i, e: (l, 0, 0))],
        out_specs=pl.BlockSpec((tm, D), lambda i, e: (i, 0)),
        out_shape=jax.ShapeDtypeStruct((T, D), F32),
        scratch_shapes=[pltpu.VMEM((tm, D), F32), pltpu.VMEM((tm, D), BF16)],
        compiler_params=_params(("parallel", "arbitrary"), 48),
        name="moe_ln",
    )(x2d, gates, w_gate, w_up, w_down, g, b)


def kernel(x, positions, w_in, attn_sinks, conv_w, a_log, dt_bias, dn_norm_w, w_out, ln1_g, ln1_b,
           router_w, router_bias, w_gate, w_up, w_down, ln2_g, ln2_b):
    batch, seq, d = x.shape
    depth = w_in.shape[0]
    T = batch * seq
    alpha = float((2 * depth) ** 0.25)

    w_main = w_in.astype(BF16)
    w_tail = jnp.concatenate(_split_bf16(
        jnp.pad(w_in[:, :, MAIN_COLS:], ((0, 0), (0, 0), (0, LANE - 2 * DN_HEADS)))), axis=-1)
    w_out_b = w_out.astype(BF16)
    w_gate_b, w_up_b, w_down_b = w_gate.astype(BF16), w_up.astype(BF16), w_down.astype(BF16)
    pad_heads = lambda t, n: jnp.pad(t, ((0, 0), (0, n - DN_HEADS)))
    alog_r = pad_heads(a_log, LANE).reshape(depth, 1, LANE)
    dtb_r = pad_heads(dt_bias, LANE).reshape(depth, 1, LANE)
    alog_c = pad_heads(a_log, 16).reshape(depth, 16, 1)
    dtb_c = pad_heads(dt_bias, 16).reshape(depth, 16, 1)
    masks = _delta_masks()
    ln1_g3, ln1_b3 = ln1_g.reshape(depth, 1, d), ln1_b.reshape(depth, 1, d)
    ln2_g3, ln2_b3 = ln2_g.reshape(depth, 1, d), ln2_b.reshape(depth, 1, d)

    cos, sin = _rope_tables(positions)
    h = x.reshape(T, d)
    for l in range(depth):
        proj, tail = _inproj(h, w_main, w_tail, l)
        tail_t = tail[:, :2 * DN_HEADS].T
        attn = _swa(proj, cos, sin, attn_sinks[l], batch, seq)
        dn = _gdn(proj, tail, tail_t, conv_w[l], alog_r[l], dtb_r[l], alog_c[l], dtb_c[l],
                  dn_norm_w[l].reshape(1, DN_HEAD_DIM), masks, batch, seq)
        h = _outproj_ln(attn, dn, w_out_b, h, ln1_g3, ln1_b3, l, alpha)
        gates = _router(h, router_w, router_bias)
        h = _moe_ln(h, gates, w_gate_b, w_up_b, w_down_b, ln2_g3, ln2_b3, l, alpha)
    return h.reshape(batch, seq, d)
```

```python
import functools

import numpy as np
import jax
import jax.numpy as jnp
from jax import lax
from jax.experimental import pallas as pl
from jax.experimental.pallas import tpu as pltpu

F32 = jnp.float32
BF16 = jnp.bfloat16
HIGHEST = lax.Precision.HIGHEST

D_MODEL = 2048
HEAD_DIM = 64
N_Q_HEADS = 16
KV_GROUP = 8
N_KV_HEADS = 2
ATT_BLOCK = 128
ROPE_THETA = 10000.0
DN_HEADS = 8
DN_HEAD_DIM = 128
CONV_WIDTH = 4
CHUNK = 64
SUPER = 256
N_EXPERTS = 16
N_GROUPS = 4
EXPERTS_PER_GROUP = 4
D_EXPERT = 512
LN_EPS = 1e-5
RMS_EPS = 1e-6
NEG = -1e30

LANE = 128
Q_BLK, K_BLK, V_BLK = 0, 8, 9
DNQ_BLK, DNK_BLK, DNV_BLK, Z_BLK = 10, 18, 26, 34
MAIN_COLS = 42 * LANE
GDN_HPB = 8
MOE_WINDOW = 2048
MOE_TM = 512
GDN_SPLIT = 2
MIB = 1024 * 1024


def _params(sem, vmem_mib):
    return pltpu.CompilerParams(dimension_semantics=sem, vmem_limit_bytes=vmem_mib * MIB)


def _split_bf16(t):
    hi = t.astype(BF16)
    return hi, (t - hi.astype(F32)).astype(BF16)


def _inproj_kernel(x_ref, w_ref, wt_ref, o_ref, t_ref, xh_ref):
    @pl.when(pl.program_id(1) == 0)
    def _():
        xh, xl = _split_bf16(x_ref[...])
        xh_ref[...] = xh
        wt = wt_ref[...]
        both = jnp.dot(xh, wt, preferred_element_type=F32) + jnp.dot(xl, wt, preferred_element_type=F32)
        t_ref[...] = both[:, :LANE] + both[:, LANE:]

    o_ref[...] = jnp.dot(xh_ref[...], w_ref[...], preferred_element_type=F32)


def _inproj(x2d, w_main, w_tail, l):
    T, K = x2d.shape
    tm, tn = min(1024, T), 768
    return pl.pallas_call(
        _inproj_kernel,
        grid=(T // tm, MAIN_COLS // tn),
        in_specs=[pl.BlockSpec((tm, K), lambda i, j: (i, 0)),
                  pl.BlockSpec((None, K, tn), lambda i, j: (l, 0, j)),
                  pl.BlockSpec((None, K, 2 * LANE), lambda i, j: (l, 0, 0))],
        out_specs=[pl.BlockSpec((tm, tn), lambda i, j: (i, j)),
                   pl.BlockSpec((tm, LANE), lambda i, j: (i, 0))],
        out_shape=[jax.ShapeDtypeStruct((T, MAIN_COLS), F32),
                   jax.ShapeDtypeStruct((T, LANE), F32)],
        scratch_shapes=[pltpu.VMEM((tm, K), BF16)],
        compiler_params=_params(("parallel", "arbitrary"), 40),
        name="inproj",
    )(x2d, w_main, w_tail)


def _rope_kernel(pos_ref, inv_ref, cos_ref, sin_ref):
    ang = pos_ref[...].astype(F32) * inv_ref[...]
    lane = lax.broadcasted_iota(jnp.int32, ang.shape, 1)
    sign = jnp.where((lane & (HEAD_DIM - 1)) < HEAD_DIM // 2, -1.0, 1.0)
    cos_ref[...] = jnp.cos(ang)
    sin_ref[...] = jnp.sin(ang) * sign


def _rope_tables(positions):
    T = positions.size
    tm = min(1024, T)
    half = HEAD_DIM // 2
    inv_freq = 1.0 / (ROPE_THETA ** (jnp.arange(0, HEAD_DIM, 2, dtype=F32) / HEAD_DIM))
    inv = jnp.tile(inv_freq, LANE // half).reshape(1, LANE)
    return pl.pallas_call(
        _rope_kernel,
        grid=(T // tm,),
        in_specs=[pl.BlockSpec((tm, 1), lambda i: (i, 0)),
                  pl.BlockSpec((1, LANE), lambda i: (0, 0))],
        out_specs=[pl.BlockSpec((tm, LANE), lambda i: (i, 0))] * 2,
        out_shape=[jax.ShapeDtypeStruct((T, LANE), F32)] * 2,
        compiler_params=_params(("parallel",), 32),
        name="rope_tables",
    )(positions.reshape(T, 1), inv)


def _rope(x, cos, sin):
    lane = lax.broadcasted_iota(jnp.int32, x.shape, 1)
    first_half = (lane & (HEAD_DIM - 1)) < HEAD_DIM // 2
    swapped = jnp.where(first_half,
                        pltpu.roll(x, LANE - HEAD_DIM // 2, axis=1),
                        pltpu.roll(x, HEAD_DIM // 2, axis=1))
    return x * cos + swapped * sin


def _swa_kernel(sink_ref, q_ref, kc_ref, kp_ref, vc_ref, vp_ref, cc_ref, sc_ref, cp_ref, sp_ref, o_ref):
    blk = pl.program_id(1)
    cos_c, sin_c = cc_ref[...], sc_ref[...]
    k_cur = _rope(kc_ref[...], cos_c, sin_c)
    k_prev = _rope(kp_ref[...], cp_ref[...], sp_ref[...])
    k_cat = jnp.concatenate([k_prev, k_cur], axis=0).astype(BF16)
    v_cat = jnp.concatenate([vp_ref[...], vc_ref[...]], axis=0).astype(BF16)

    qi = lax.broadcasted_iota(jnp.int32, (ATT_BLOCK, 2 * ATT_BLOCK), 0)
    kj = lax.broadcasted_iota(jnp.int32, (ATT_BLOCK, 2 * ATT_BLOCK), 1)
    first_key = jnp.where(blk > 0, 0, ATT_BLOCK)
    valid = (kj > qi) & (kj <= qi + ATT_BLOCK) & (kj >= first_key)

    lane = lax.broadcasted_iota(jnp.int32, (ATT_BLOCK, LANE), 1)
    low_head = lane < HEAD_DIM
    pairs_per_kv = KV_GROUP // 2
    group = range(KV_GROUP)
    for kvh in range(N_KV_HEADS):
        k_h = k_cat[:, kvh * HEAD_DIM:(kvh + 1) * HEAD_DIM]
        v_h = v_cat[:, kvh * HEAD_DIM:(kvh + 1) * HEAD_DIM]
        k_dup = jnp.concatenate([k_h, k_h], axis=1)
        v_dup = jnp.concatenate([v_h, v_h], axis=1)
        q_rows = []
        for pp in range(pairs_per_kv):
            pair = kvh * pairs_per_kv + pp
            qp = _rope(q_ref[:, pair * LANE:(pair + 1) * LANE], cos_c, sin_c) * (HEAD_DIM ** -0.5)
            q_rows += [jnp.where(low_head, qp, 0.0), jnp.where(low_head, 0.0, qp)]
        q_stack = jnp.concatenate(q_rows, axis=0).astype(BF16)
        s_all = lax.dot_general(q_stack, k_dup, (((1,), (1,)), ((), ())), preferred_element_type=F32)
        sinks = [sink_ref[kvh * KV_GROUP + g] for g in group]
        s = [jnp.where(valid, s_all[g * ATT_BLOCK:(g + 1) * ATT_BLOCK], NEG) for g in group]
        m = [jnp.maximum(jnp.max(s[g], axis=-1, keepdims=True), sinks[g]) for g in group]
        p = [jnp.exp(s[g] - m[g]) for g in group]
        inv = [1.0 / (jnp.sum(p[g], axis=-1, keepdims=True) + jnp.exp(sinks[g] - m[g])) for g in group]
        p_stack = jnp.concatenate([p[g].astype(BF16) for g in group], axis=0)
        o_all = jnp.dot(p_stack, v_dup, preferred_element_type=F32)
        for pp in range(pairs_per_kv):
            pair = kvh * pairs_per_kv + pp
            ge, go = 2 * pp, 2 * pp + 1
            o_even = o_all[ge * ATT_BLOCK:(ge + 1) * ATT_BLOCK] * inv[ge]
            o_odd = o_all[go * ATT_BLOCK:(go + 1) * ATT_BLOCK] * inv[go]
            o_ref[:, pair * LANE:(pair + 1) * LANE] = jnp.where(low_head, o_even, o_odd).astype(o_ref.dtype)


def _swa(proj, cos, sin, sinks, batch, seq):
    T = proj.shape[0]
    nb = seq // ATT_BLOCK
    cur = lambda b, i: b * nb + i
    prev = lambda b, i: b * nb + jnp.maximum(i - 1, 0)
    blk = lambda rows_fn, col: pl.BlockSpec((ATT_BLOCK, LANE), lambda b, i: (rows_fn(b, i), col))
    return pl.pallas_call(
        _swa_kernel,
        grid=(batch, nb),
        in_specs=[pl.BlockSpec(memory_space=pltpu.SMEM),
                  pl.BlockSpec((ATT_BLOCK, N_Q_HEADS * HEAD_DIM), lambda b, i: (cur(b, i), Q_BLK // (N_Q_HEADS * HEAD_DIM // LANE))),
                  blk(cur, K_BLK), blk(prev, K_BLK), blk(cur, V_BLK), blk(prev, V_BLK),
                  blk(cur, 0), blk(cur, 0), blk(prev, 0), blk(prev, 0)],
        out_specs=pl.BlockSpec((ATT_BLOCK, N_Q_HEADS * HEAD_DIM), lambda b, i: (cur(b, i), 0)),
        out_shape=jax.ShapeDtypeStruct((T, N_Q_HEADS * HEAD_DIM), BF16),
        compiler_params=_params(("parallel", "arbitrary"), 32),
        name="swa",
    )(sinks, proj, proj, proj, proj, proj, cos, sin, cos, sin)


def _delta_masks():
    i = np.arange(CHUNK)[:, None]
    j = (np.arange(SUPER) % CHUNK)[None, :]
    ms = [i > j, i >= j, ((i // 4) == (j // 4)) & (i > j)]
    s = 4
    while s < CHUNK:
        ms.append(((i // (2 * s)) == (j // (2 * s))) & ((i // s) % 2 == 1) & ((j // s) % 2 == 0))
        s *= 2
    ms.append(i == j)
    r = np.arange(SUPER)[:, None]
    c = np.arange(SUPER)[None, :]
    same = (r // CHUNK) == (c // CHUNK)
    return (jnp.asarray(np.stack(ms).astype(np.float32)),
            jnp.asarray((same & (r >= c)).astype(np.float32)),
            jnp.asarray(same.astype(np.float32), dtype=BF16))


def _softplus(x):
    return jnp.maximum(x, 0.0) + jnp.log(1.0 + jnp.exp(-jnp.abs(x)))


def _bdot(a, b):
    return jnp.dot(a.astype(BF16), b.astype(BF16), preferred_element_type=F32)


def _gdn_kernel(*refs):
    pieces = GDN_HPB // GDN_SPLIT
    q_refs, k_refs, v_refs, hq_refs, hk_refs, hv_refs, z_refs = (refs[i * pieces:(i + 1) * pieces] for i in range(7))
    (cwq_ref, cwk_ref, cwv_ref, tail_ref, tailt_ref, alog_r_ref, dtb_r_ref, alog_c_ref, dtb_c_ref,
     nw_ref, wmask_ref, causal_ref, same_ref, o_ref, state_ref) = refs[7 * pieces:]
    head0 = pl.program_id(1) * GDN_HPB
    step = pl.program_id(2)
    heads = range(GDN_HPB)
    sls = [slice(j * LANE, (j + 1) * LANE) for j in heads]

    @pl.when(step == 0)
    def _():
        state_ref[...] = jnp.zeros_like(state_ref)

    def head_block(piece_refs, j):
        return piece_refs[j // GDN_SPLIT][:, (j % GDN_SPLIT) * LANE:(j % GDN_SPLIT + 1) * LANE]

    def conv_silu(x_refs, halo_refs, w_ref, j):
        halo = jnp.where(step > 0, head_block(halo_refs, j), 0.0)
        xx = jnp.concatenate([halo, head_block(x_refs, j)], axis=0)
        w = w_ref[:, sls[j]]
        y = xx[8:] * w[CONV_WIDTH - 1:CONV_WIDTH]
        for j in range(CONV_WIDTH - 1):
            y = y + pltpu.roll(xx, CONV_WIDTH - 1 - j, axis=0)[8:] * w[j:j + 1]
        return y * jax.nn.sigmoid(y)

    def l2norm(x, scale):
        return x * (lax.rsqrt(jnp.sum(x * x, axis=-1, keepdims=True) + RMS_EPS) * scale)

    m_causal = causal_ref[...]
    tail = tail_ref[...]
    lane = lax.broadcasted_iota(jnp.int32, tail.shape, 1)
    g_cols = -jnp.exp(alog_r_ref[...]) * _softplus(tail + dtb_r_ref[...])
    gc_cols = jnp.dot(m_causal, g_cols, preferred_element_type=F32, precision=HIGHEST)
    tail_t = tailt_ref[...]
    sub = lax.broadcasted_iota(jnp.int32, tail_t.shape, 0)
    g_rows = -jnp.exp(alog_c_ref[...]) * _softplus(tail_t + dtb_c_ref[...])
    gc_rows = lax.dot_general(g_rows, m_causal, (((1,), (1,)), ((), ())),
                              preferred_element_type=F32, precision=HIGHEST)

    lane_r = lax.broadcasted_iota(jnp.int32, (1, SUPER), 1)
    n_chunks = SUPER // CHUNK
    chunk_of_lane = lax.broadcasted_iota(jnp.int32, (CHUNK, SUPER), 1) >> 6
    same_bd = same_ref[...]
    w_strict, w_causal, w_base, w_eye = wmask_ref[0], wmask_ref[1], wmask_ref[2], wmask_ref[3 + 4]

    def diag_blocks_wide(full):
        out = full[(n_chunks - 1) * CHUNK:]
        for c in range(n_chunks - 2, -1, -1):
            out = jnp.where(chunk_of_lane == c, full[c * CHUNK:(c + 1) * CHUNK], out)
        return out

    def column_wide(col):
        out = jnp.broadcast_to(col[(n_chunks - 1) * CHUNK:], (CHUNK, SUPER))
        for c in range(n_chunks - 2, -1, -1):
            out = jnp.where(chunk_of_lane == c, jnp.broadcast_to(col[c * CHUNK:(c + 1) * CHUNK], (CHUNK, SUPER)), out)
        return out

    def block_diag(wide):
        return jnp.concatenate([wide.astype(BF16)] * n_chunks, axis=0) * same_bd

    def wdot(a_wide, b_wide):
        return jnp.dot(a_wide.astype(BF16), block_diag(b_wide), preferred_element_type=F32)

    q = [l2norm(conv_silu(q_refs, hq_refs, cwq_ref, j), DN_HEAD_DIM ** -0.5) for j in heads]
    k = [l2norm(conv_silu(k_refs, hk_refs, cwk_ref, j), 1.0) for j in heads]
    v = [conv_silu(v_refs, hv_refs, cwv_ref, j) for j in heads]
    gc_col = [jnp.sum(jnp.where(lane == head0 + j, gc_cols, 0.0), axis=-1, keepdims=True) for j in heads]
    beta = [jax.nn.sigmoid(jnp.sum(jnp.where(lane == DN_HEADS + head0 + j, tail, 0.0), axis=-1, keepdims=True))
            for j in heads]
    gc_row = [jnp.sum(jnp.where(sub == head0 + j, gc_rows, 0.0), axis=0, keepdims=True) for j in heads]
    decay = [jnp.exp(jnp.where(w_causal > 0, column_wide(gc_col[j]) - gc_row[j], NEG)) for j in heads]
    kb = [k[j] * beta[j] for j in heads]
    vb = [v[j] * beta[j] for j in heads]
    kq = [lax.dot_general(jnp.concatenate([kb[j], q[j]], axis=0).astype(BF16), k[j].astype(BF16),
                          (((1,), (1,)), ((), ())), preferred_element_type=F32) for j in heads]
    lmat = [diag_blocks_wide(kq[j][:SUPER]) * decay[j] * w_strict for j in heads]
    a_qk = [diag_blocks_wide(kq[j][SUPER:]) * decay[j] for j in heads]

    l4 = [lmat[j] * w_base for j in heads]
    l4sq = [wdot(l4[j], l4[j]) for j in heads]
    l4cu = [wdot(l4[j], l4sq[j]) for j in heads]
    tinv = [w_eye - l4[j] + l4sq[j] - l4cu[j] for j in heads]
    for lvl in range(4):
        ct = [wdot(lmat[j] * wmask_ref[3 + lvl], tinv[j]) for j in heads]
        tinv = [tinv[j] - wdot(tinv[j], ct[j]) for j in heads]

    e_col = [jnp.exp(gc_col[j]) for j in heads]
    uw = [jnp.dot(block_diag(tinv[j]), jnp.concatenate([vb[j], kb[j] * e_col[j]], axis=1).astype(BF16),
                  preferred_element_type=F32) for j in heads]
    qg = [q[j] * e_col[j] for j in heads]
    kd_t = []
    for j in heads:
        gc_last_row = jnp.zeros((1, SUPER), F32)
        for c in range(SUPER // CHUNK):
            last = gc_row[j][:, (c + 1) * CHUNK - 1:(c + 1) * CHUNK]
            gc_last_row = jnp.where((lane_r >> 6) == c, last, gc_last_row)
        kd_t.append(k[j].T * jnp.exp(gc_last_row - gc_row[j]))

    ak = [jnp.concatenate([a_qk[j], kd_t[j]], axis=0).astype(BF16) for j in heads]
    state = [state_ref[j] for j in heads]
    outs = [[] for _ in heads]
    for c in range(n_chunks):
        r0 = c * CHUNK
        rows = slice(r0, r0 + CHUNK)
        ws = [_bdot(jnp.concatenate([uw[j][rows, LANE:], qg[j][rows]], axis=0), state[j]) for j in heads]
        v_new = [uw[j][rows, :LANE] - ws[j][:CHUNK] for j in heads]
        v_pad = [jnp.concatenate(([jnp.zeros((r0, LANE), BF16)] if r0 else []) + [v_new[j].astype(BF16)]
                                 + ([jnp.zeros((SUPER - r0 - CHUNK, LANE), BF16)] if r0 + CHUNK < SUPER else []),
                                 axis=0) for j in heads]
        both = [jnp.dot(ak[j], v_pad[j], preferred_element_type=F32) for j in heads]
        for j in heads:
            outs[j].append(ws[j][CHUNK:] + both[j][:CHUNK])
        state = [state[j] * jnp.exp(gc_row[j][:, r0 + CHUNK - 1:r0 + CHUNK]) + both[j][CHUNK:] for j in heads]
    for j in heads:
        state_ref[j] = state[j]
        o = jnp.concatenate(outs[j], axis=0)
        z = head_block(z_refs, j)
        o = o * lax.rsqrt(jnp.mean(o * o, axis=-1, keepdims=True) + RMS_EPS) * nw_ref[...] * (z * jax.nn.sigmoid(z))
        o_ref[:, sls[j]] = o.astype(o_ref.dtype)


def _gdn(proj, tail, tail_t, conv_w, alog_r, dtb_r, alog_c, dtb_c, norm_w, masks, batch, seq):
    T = proj.shape[0]
    ns = seq // SUPER
    hpb = GDN_HPB
    halo_per_super = SUPER // 8
    row = lambda b, i: b * ns + i
    halo_row = lambda b, i: jnp.maximum((b * ns + i) * halo_per_super - 1, 0)
    pieces = hpb // GDN_SPLIT
    piece_col = lambda col, h, m: col // GDN_SPLIT + h * pieces + m
    cur = lambda col: [pl.BlockSpec((SUPER, GDN_SPLIT * LANE), lambda b, h, i, m=m: (row(b, i), piece_col(col, h, m)))
                       for m in range(pieces)]
    halo = lambda col: [pl.BlockSpec((8, GDN_SPLIT * LANE), lambda b, h, i, m=m: (halo_row(b, i), piece_col(col, h, m)))
                        for m in range(pieces)]
    cw = lambda col: pl.BlockSpec((CONV_WIDTH, hpb * LANE), lambda b, h, i: (0, col // hpb + h))
    const2 = lambda shape: pl.BlockSpec(shape, lambda b, h, i: (0, 0))
    return pl.pallas_call(
        _gdn_kernel,
        grid=(batch, DN_HEADS // hpb, ns),
        in_specs=[*cur(DNQ_BLK), *cur(DNK_BLK), *cur(DNV_BLK),
                  *halo(DNQ_BLK), *halo(DNK_BLK), *halo(DNV_BLK), *cur(Z_BLK),
                  cw(0), cw(DN_HEADS), cw(2 * DN_HEADS),
                  pl.BlockSpec((SUPER, LANE), lambda b, h, i: (row(b, i), 0)),
                  pl.BlockSpec((16, SUPER), lambda b, h, i: (0, row(b, i))),
                  const2((1, LANE)), const2((1, LANE)), const2((16, 1)), const2((16, 1)),
                  const2((1, LANE)),
                  pl.BlockSpec(masks[0].shape, lambda b, h, i: (0, 0, 0)),
                  const2((SUPER, SUPER)), const2((SUPER, SUPER))],
        out_specs=pl.BlockSpec((SUPER, hpb * LANE), lambda b, h, i: (row(b, i), h)),
        out_shape=jax.ShapeDtypeStruct((T, DN_HEADS * DN_HEAD_DIM), BF16),
        scratch_shapes=[pltpu.VMEM((hpb, DN_HEAD_DIM, DN_HEAD_DIM), F32)],
        compiler_params=_params(("parallel", "parallel", "arbitrary"), 40),
        name="gdn",
    )(*([proj] * (7 * pieces)), conv_w, conv_w, conv_w, tail, tail_t,
      alog_r, dtb_r, alog_c, dtb_c, norm_w, *masks)


def _layer_norm(y, g, b):
    mu = jnp.mean(y, axis=-1, keepdims=True)
    yc = y - mu
    var = jnp.mean(yc * yc, axis=-1, keepdims=True)
    return yc * lax.rsqrt(var + LN_EPS) * g + b


def _outproj_kernel(alpha, a_ref, d_ref, wa_ref, wd_ref, x_ref, g_ref, b_ref, o_ref, ob_ref):
    mix = jnp.dot(a_ref[...], wa_ref[...], preferred_element_type=F32)
    mix = mix + jnp.dot(d_ref[...], wd_ref[...], preferred_element_type=F32)
    out = _layer_norm(alpha * x_ref[...] + mix, g_ref[...], b_ref[...])
    o_ref[...] = out
    ob_ref[...] = out.astype(BF16)


def _outproj_ln(attn, dn, w_out, x2d, g, b, l, alpha):
    T, D = x2d.shape
    tm = 256
    half = attn.shape[1]
    return pl.pallas_call(
        functools.partial(_outproj_kernel, alpha),
        grid=(T // tm,),
        in_specs=[pl.BlockSpec((tm, half), lambda i: (i, 0)),
                  pl.BlockSpec((tm, half), lambda i: (i, 0)),
                  pl.BlockSpec((None, half, D), lambda i: (l, 0, 0)),
                  pl.BlockSpec((None, half, D), lambda i: (l, 1, 0)),
                  pl.BlockSpec((tm, D), lambda i: (i, 0)),
                  pl.BlockSpec((None, 1, D), lambda i: (l, 0, 0)),
                  pl.BlockSpec((None, 1, D), lambda i: (l, 0, 0))],
        out_specs=[pl.BlockSpec((tm, D), lambda i: (i, 0))] * 2,
        out_shape=[jax.ShapeDtypeStruct((T, D), F32), jax.ShapeDtypeStruct((T, D), BF16)],
        compiler_params=_params(("parallel",), 48),
        name="outproj_ln",
    )(attn, dn, w_out, w_out, x2d, g, b)


def _router_kernel(x_ref, w_ref, bias_ref, gates_ref, grp_ref):
    xh, xl = _split_bf16(x_ref[...])
    w_parts = jnp.concatenate(_split_bf16(w_ref[...]), axis=1)
    both = jnp.dot(xh, w_parts, preferred_element_type=F32) + jnp.dot(xl, w_parts, preferred_element_type=F32)
    logits = both[:, :N_EXPERTS] + both[:, N_EXPERTS:]
    scores = jax.nn.sigmoid(logits)
    sel = scores + bias_ref[...]
    e = lax.broadcasted_iota(jnp.int32, sel.shape, 1)
    big = jnp.int32(N_EXPERTS)

    def top2(vals):
        m1 = jnp.max(vals, axis=-1, keepdims=True)
        i1 = jnp.min(jnp.where(vals == m1, e, big), axis=-1, keepdims=True)
        rest = jnp.where(e == i1, -jnp.inf, vals)
        m2 = jnp.max(rest, axis=-1, keepdims=True)
        i2 = jnp.min(jnp.where(rest == m2, e, big), axis=-1, keepdims=True)
        return m1, i1, m2, i2

    best = None
    for grp in range(N_GROUPS):
        m1, i1, m2, i2 = top2(jnp.where((e >> 2) == grp, sel, -jnp.inf))
        gsum = m1 + m2
        if best is None:
            best = (gsum, i1, i2)
        else:
            take = gsum > best[0]
            best = (jnp.where(take, gsum, best[0]), jnp.where(take, i1, best[1]), jnp.where(take, i2, best[2]))
    _, e1, e2 = best
    w1 = jnp.sum(jnp.where(e == e1, scores, 0.0), axis=-1, keepdims=True)
    w2 = jnp.sum(jnp.where(e == e2, scores, 0.0), axis=-1, keepdims=True)
    tot = w1 + w2
    gates_ref[...] = jnp.where(e == e1, w1 / tot, 0.0) + jnp.where(e == e2, w2 / tot, 0.0)
    grp_ref[...] = e1 >> 2


def _router(x2d, router_w, router_bias):
    T, D = x2d.shape
    tm = 512
    return pl.pallas_call(
        _router_kernel,
        grid=(T // tm,),
        in_specs=[pl.BlockSpec((tm, D), lambda i: (i, 0)),
                  pl.BlockSpec((D, N_EXPERTS), lambda i: (0, 0)),
                  pl.BlockSpec((1, N_EXPERTS), lambda i: (0, 0))],
        out_specs=[pl.BlockSpec((tm, N_EXPERTS), lambda i: (i, 0)), pl.BlockSpec((tm, 1), lambda i: (i, 0))],
        out_shape=[jax.ShapeDtypeStruct((T, N_EXPERTS), F32), jax.ShapeDtypeStruct((T, 1), jnp.int32)],
        compiler_params=_params(("parallel",), 32),
        name="router",
    )(x2d, router_w, router_bias.reshape(1, N_EXPERTS))


def _one_hot_rows(idx_col, width):
    lane = lax.broadcasted_iota(jnp.int32, (idx_col.shape[0], width), 1)
    return jnp.where(lane == idx_col, 1.0, 0.0).astype(BF16)


def _dispatch_kernel(idx_ref, xb_ref, gates_ref, xs_ref, gs_ref):
    p = _one_hot_rows(idx_ref[...], MOE_WINDOW)
    xs_ref[...] = jnp.dot(p, xb_ref[...], preferred_element_type=F32).astype(BF16)
    g_parts = jnp.concatenate(_split_bf16(gates_ref[...]), axis=1)
    both = jnp.dot(p, g_parts, preferred_element_type=F32)
    gs_ref[...] = both[:, :N_EXPERTS] + both[:, N_EXPERTS:]


def _dispatch(order_col, xb, gates):
    T, D = xb.shape
    tr = MOE_TM
    per = MOE_WINDOW // tr
    return pl.pallas_call(
        _dispatch_kernel,
        grid=(T // MOE_WINDOW, per),
        in_specs=[pl.BlockSpec((tr, 1), lambda w, j: (w * per + j, 0)),
                  pl.BlockSpec((MOE_WINDOW, D), lambda w, j: (w, 0)),
                  pl.BlockSpec((MOE_WINDOW, N_EXPERTS), lambda w, j: (w, 0))],
        out_specs=[pl.BlockSpec((tr, D), lambda w, j: (w * per + j, 0)),
                   pl.BlockSpec((tr, N_EXPERTS), lambda w, j: (w * per + j, 0))],
        out_shape=[jax.ShapeDtypeStruct((T, D), BF16), jax.ShapeDtypeStruct((T, N_EXPERTS), F32)],
        compiler_params=_params(("parallel", "arbitrary"), 48),
        name="moe_dispatch",
    )(order_col, xb, gates)


def _moe_kernel(act_ref, fetch_ref, x_ref, gates_ref, wg_ref, wu_ref, wd_ref, o_ref, acc_ref):
    tile, ex = pl.program_id(0), pl.program_id(1)

    @pl.when(ex == 0)
    def _():
        acc_ref[...] = jnp.zeros_like(acc_ref)

    @pl.when(act_ref[tile * N_EXPERTS + ex] > 0)
    def _():
        xb = x_ref[...]
        a = jnp.dot(xb, wg_ref[...], preferred_element_type=F32)
        u = jnp.dot(xb, wu_ref[...], preferred_element_type=F32)
        gates = gates_ref[...]
        lane = lax.broadcasted_iota(jnp.int32, gates.shape, 1)
        gate = jnp.sum(jnp.where(lane == ex, gates, 0.0), axis=-1, keepdims=True)
        hidden = (a * jax.nn.sigmoid(a)) * u * gate
        acc_ref[...] += jnp.dot(hidden.astype(BF16), wd_ref[...], preferred_element_type=F32)

    @pl.when(ex == N_EXPERTS - 1)
    def _():
        o_ref[...] = acc_ref[...].astype(o_ref.dtype)


def _moe_ffn(active, fetch, xs, gates_s, w_gate, w_up, w_down, l):
    T, D = xs.shape
    tm = MOE_TM
    wmap = lambda i, e, act, fetch: (l, fetch[i * N_EXPERTS + e], 0, 0)
    return pl.pallas_call(
        _moe_kernel,
        grid_spec=pltpu.PrefetchScalarGridSpec(
            num_scalar_prefetch=2,
            grid=(T // tm, N_EXPERTS),
            in_specs=[pl.BlockSpec((tm, D), lambda i, e, act, fetch: (i, 0)),
                      pl.BlockSpec((tm, N_EXPERTS), lambda i, e, act, fetch: (i, 0)),
                      pl.BlockSpec((None, None, D, D_EXPERT), wmap),
                      pl.BlockSpec((None, None, D, D_EXPERT), wmap),
                      pl.BlockSpec((None, None, D_EXPERT, D), wmap)],
            out_specs=pl.BlockSpec((tm, D), lambda i, e, act, fetch: (i, 0)),
            scratch_shapes=[pltpu.VMEM((tm, D), F32)]),
        out_shape=jax.ShapeDtypeStruct((T, D), BF16),
        compiler_params=_params(("parallel", "arbitrary"), 48),
        name="moe_ffn",
    )(active, fetch, xs, gates_s, w_gate, w_up, w_down)


def _combine_kernel(alpha, idx_ref, f_ref, x_ref, g_ref, b_ref, o_ref):
    p = _one_hot_rows(idx_ref[...], MOE_WINDOW)
    ffn = jnp.dot(p, f_ref[...], preferred_element_type=F32)
    o_ref[...] = _layer_norm(alpha * x_ref[...] + ffn, g_ref[...], b_ref[...])


def _combine_ln(dest_col, ffn_s, x2d, g, b, l, alpha):
    T, D = x2d.shape
    tr = MOE_TM
    per = MOE_WINDOW // tr
    return pl.pallas_call(
        functools.partial(_combine_kernel, alpha),
        grid=(T // MOE_WINDOW, per),
        in_specs=[pl.BlockSpec((tr, 1), lambda w, j: (w * per + j, 0)),
                  pl.BlockSpec((MOE_WINDOW, D), lambda w, j: (w, 0)),
                  pl.BlockSpec((tr, D), lambda w, j: (w * per + j, 0)),
                  pl.BlockSpec((None, 1, D), lambda w, j: (l, 0, 0)),
                  pl.BlockSpec((None, 1, D), lambda w, j: (l, 0, 0))],
        out_specs=pl.BlockSpec((tr, D), lambda w, j: (w * per + j, 0)),
        out_shape=jax.ShapeDtypeStruct((T, D), F32),
        compiler_params=_params(("parallel", "arbitrary"), 48),
        name="moe_combine_ln",
    )(dest_col, ffn_s, x2d, g, b)


def _dispatch_plan(grp, T):
    nw = T // MOE_WINDOW
    g2 = grp.reshape(nw, MOE_WINDOW)
    order = jnp.argsort(g2, axis=1, stable=True).astype(jnp.int32)
    dest = jnp.argsort(order, axis=1).astype(jnp.int32)
    grp_sorted = jnp.take_along_axis(g2, order, axis=1).reshape(T // MOE_TM, MOE_TM)
    present = (grp_sorted[:, :, None] == jnp.arange(N_GROUPS, dtype=jnp.int32)[None, None, :]).any(axis=1)
    active = jnp.repeat(present, EXPERTS_PER_GROUP, axis=1).astype(jnp.int32).reshape(-1)
    steps = jnp.arange(active.size, dtype=jnp.int32)
    last_active = lax.cummax(jnp.where(active > 0, steps, -1))
    fetch = jnp.where(last_active >= 0, last_active % N_EXPERTS, 0).astype(jnp.int32)
    return order.reshape(T, 1), dest.reshape(T, 1), active, fetch


def kernel(x, positions, w_in, attn_sinks, conv_w, a_log, dt_bias, dn_norm_w, w_out, ln1_g, ln1_b,
           router_w, router_bias, w_gate, w_up, w_down, ln2_g, ln2_b):
    batch, seq, d = x.shape
    depth = w_in.shape[0]
    T = batch * seq
    alpha = float((2 * depth) ** 0.25)

    w_main = w_in.astype(BF16)
    w_tail = jnp.concatenate(_split_bf16(
        jnp.pad(w_in[:, :, MAIN_COLS:], ((0, 0), (0, 0), (0, LANE - 2 * DN_HEADS)))), axis=-1)
    w_out_b = w_out.astype(BF16)
    w_gate_b, w_up_b, w_down_b = w_gate.astype(BF16), w_up.astype(BF16), w_down.astype(BF16)
    pad_heads = lambda t, n: jnp.pad(t, ((0, 0), (0, n - DN_HEADS)))
    alog_r = pad_heads(a_log, LANE).reshape(depth, 1, LANE)
    dtb_r = pad_heads(dt_bias, LANE).reshape(depth, 1, LANE)
    alog_c = pad_heads(a_log, 16).reshape(depth, 16, 1)
    dtb_c = pad_heads(dt_bias, 16).reshape(depth, 16, 1)
    masks = _delta_masks()
    ln1_g3, ln1_b3 = ln1_g.reshape(depth, 1, d), ln1_b.reshape(depth, 1, d)
    ln2_g3, ln2_b3 = ln2_g.reshape(depth, 1, d), ln2_b.reshape(depth, 1, d)

    cos, sin = _rope_tables(positions)
    h = x.reshape(T, d)
    for l in range(depth):
        proj, tail = _inproj(h, w_main, w_tail, l)
        tail_t = tail[:, :2 * DN_HEADS].T
        attn = _swa(proj, cos, sin, attn_sinks[l], batch, seq)
        dn = _gdn(proj, tail, tail_t, conv_w[l], alog_r[l], dtb_r[l], alog_c[l], dtb_c[l],
                  dn_norm_w[l].reshape(1, DN_HEAD_DIM), masks, batch, seq)
        h, hb = _outproj_ln(attn, dn, w_out_b, h, ln1_g3, ln1_b3, l, alpha)
        gates, grp = _router(h, router_w, router_bias)
        order_col, dest_col, active, fetch = _dispatch_plan(grp, T)
        xs, gates_s = _dispatch(order_col, hb, gates)
        ffn_s = _moe_ffn(active, fetch, xs, gates_s, w_gate_b, w_up_b, w_down_b, l)
        h = _combine_ln(dest_col, ffn_s, h, ln2_g3, ln2_b3, l, alpha)
    return h.reshape(batch, seq, d)
```

```python
import functools

import numpy as np
import jax
import jax.numpy as jnp
from jax import lax
from jax.experimental import pallas as pl
from jax.experimental.pallas import tpu as pltpu

F32 = jnp.float32
BF16 = jnp.bfloat16
HIGHEST = lax.Precision.HIGHEST

D_MODEL = 2048
HEAD_DIM = 64
N_Q_HEADS = 16
KV_GROUP = 8
N_KV_HEADS = 2
ATT_BLOCK = 128
ROPE_THETA = 10000.0
DN_HEADS = 8
DN_HEAD_DIM = 128
CONV_WIDTH = 4
CHUNK = 64
SUPER = 256
N_EXPERTS = 16
N_GROUPS = 4
EXPERTS_PER_GROUP = 4
D_EXPERT = 512
LN_EPS = 1e-5
RMS_EPS = 1e-6
NEG = -1e30

LANE = 128
Q_BLK, K_BLK, V_BLK = 0, 8, 9
DNQ_BLK, DNK_BLK, DNV_BLK, Z_BLK = 10, 18, 26, 34
MAIN_COLS = 42 * LANE
GDN_HPB = 8
MOE_WINDOW = 2048
MOE_TM = 512
MOE_EPS = 2
GDN_SPLIT = 2
MIB = 1024 * 1024


def _params(sem, vmem_mib):
    return pltpu.CompilerParams(dimension_semantics=sem, vmem_limit_bytes=vmem_mib * MIB)


def _split_bf16(t):
    hi = t.astype(BF16)
    return hi, (t - hi.astype(F32)).astype(BF16)


def _inproj_kernel(x_ref, w_ref, wt_ref, o_ref, t_ref, xh_ref):
    @pl.when(pl.program_id(1) == 0)
    def _():
        xh, xl = _split_bf16(x_ref[...])
        xh_ref[...] = xh
        wt = wt_ref[...]
        both = jnp.dot(xh, wt, preferred_element_type=F32) + jnp.dot(xl, wt, preferred_element_type=F32)
        t_ref[...] = both[:, :LANE] + both[:, LANE:]

    o_ref[...] = jnp.dot(xh_ref[...], w_ref[...], preferred_element_type=F32)


def _inproj(x2d, w_main, w_tail, l):
    T, K = x2d.shape
    tm, tn = min(1024, T), 768
    return pl.pallas_call(
        _inproj_kernel,
        grid=(T // tm, MAIN_COLS // tn),
        in_specs=[pl.BlockSpec((tm, K), lambda i, j: (i, 0)),
                  pl.BlockSpec((None, K, tn), lambda i, j: (l, 0, j)),
                  pl.BlockSpec((None, K, 2 * LANE), lambda i, j: (l, 0, 0))],
        out_specs=[pl.BlockSpec((tm, tn), lambda i, j: (i, j)),
                   pl.BlockSpec((tm, LANE), lambda i, j: (i, 0))],
        out_shape=[jax.ShapeDtypeStruct((T, MAIN_COLS), F32),
                   jax.ShapeDtypeStruct((T, LANE), F32)],
        scratch_shapes=[pltpu.VMEM((tm, K), BF16)],
        compiler_params=_params(("parallel", "arbitrary"), 40),
        name="inproj",
    )(x2d, w_main, w_tail)


def _rope_kernel(pos_ref, inv_ref, cos_ref, sin_ref):
    ang = pos_ref[...].astype(F32) * inv_ref[...]
    lane = lax.broadcasted_iota(jnp.int32, ang.shape, 1)
    sign = jnp.where((lane & (HEAD_DIM - 1)) < HEAD_DIM // 2, -1.0, 1.0)
    cos_ref[...] = jnp.cos(ang)
    sin_ref[...] = jnp.sin(ang) * sign


def _rope_tables(positions):
    T = positions.size
    tm = min(1024, T)
    half = HEAD_DIM // 2
    inv_freq = 1.0 / (ROPE_THETA ** (jnp.arange(0, HEAD_DIM, 2, dtype=F32) / HEAD_DIM))
    inv = jnp.tile(inv_freq, LANE // half).reshape(1, LANE)
    return pl.pallas_call(
        _rope_kernel,
        grid=(T // tm,),
        in_specs=[pl.BlockSpec((tm, 1), lambda i: (i, 0)),
                  pl.BlockSpec((1, LANE), lambda i: (0, 0))],
        out_specs=[pl.BlockSpec((tm, LANE), lambda i: (i, 0))] * 2,
        out_shape=[jax.ShapeDtypeStruct((T, LANE), F32)] * 2,
        compiler_params=_params(("parallel",), 32),
        name="rope_tables",
    )(positions.reshape(T, 1), inv)


def _rope(x, cos, sin):
    lane = lax.broadcasted_iota(jnp.int32, x.shape, 1)
    first_half = (lane & (HEAD_DIM - 1)) < HEAD_DIM // 2
    swapped = jnp.where(first_half,
                        pltpu.roll(x, LANE - HEAD_DIM // 2, axis=1),
                        pltpu.roll(x, HEAD_DIM // 2, axis=1))
    return x * cos + swapped * sin


def _swa_kernel(sink_ref, q_ref, kc_ref, kp_ref, vc_ref, vp_ref, cc_ref, sc_ref, cp_ref, sp_ref, o_ref):
    blk = pl.program_id(1)
    cos_c, sin_c = cc_ref[...], sc_ref[...]
    k_cur = _rope(kc_ref[...], cos_c, sin_c)
    k_prev = _rope(kp_ref[...], cp_ref[...], sp_ref[...])
    k_cat = jnp.concatenate([k_prev, k_cur], axis=0).astype(BF16)
    v_cat = jnp.concatenate([vp_ref[...], vc_ref[...]], axis=0).astype(BF16)

    qi = lax.broadcasted_iota(jnp.int32, (ATT_BLOCK, 2 * ATT_BLOCK), 0)
    kj = lax.broadcasted_iota(jnp.int32, (ATT_BLOCK, 2 * ATT_BLOCK), 1)
    first_key = jnp.where(blk > 0, 0, ATT_BLOCK)
    valid = (kj > qi) & (kj <= qi + ATT_BLOCK) & (kj >= first_key)

    lane = lax.broadcasted_iota(jnp.int32, (ATT_BLOCK, LANE), 1)
    low_head = lane < HEAD_DIM
    pairs_per_kv = KV_GROUP // 2
    group = range(KV_GROUP)
    for kvh in range(N_KV_HEADS):
        k_h = k_cat[:, kvh * HEAD_DIM:(kvh + 1) * HEAD_DIM]
        v_h = v_cat[:, kvh * HEAD_DIM:(kvh + 1) * HEAD_DIM]
        k_dup = jnp.concatenate([k_h, k_h], axis=1)
        v_dup = jnp.concatenate([v_h, v_h], axis=1)
        q_rows = []
        for pp in range(pairs_per_kv):
            pair = kvh * pairs_per_kv + pp
            qp = _rope(q_ref[:, pair * LANE:(pair + 1) * LANE], cos_c, sin_c) * (HEAD_DIM ** -0.5)
            q_rows += [jnp.where(low_head, qp, 0.0), jnp.where(low_head, 0.0, qp)]
        q_stack = jnp.concatenate(q_rows, axis=0).astype(BF16)
        s_all = lax.dot_general(q_stack, k_dup, (((1,), (1,)), ((), ())), preferred_element_type=F32)
        sinks = [sink_ref[kvh * KV_GROUP + g] for g in group]
        s = [jnp.where(valid, s_all[g * ATT_BLOCK:(g + 1) * ATT_BLOCK], NEG) for g in group]
        m = [jnp.maximum(jnp.max(s[g], axis=-1, keepdims=True), sinks[g]) for g in group]
        p = [jnp.exp(s[g] - m[g]) for g in group]
        inv = [1.0 / (jnp.sum(p[g], axis=-1, keepdims=True) + jnp.exp(sinks[g] - m[g])) for g in group]
        p_stack = jnp.concatenate([p[g].astype(BF16) for g in group], axis=0)
        o_all = jnp.dot(p_stack, v_dup, preferred_element_type=F32)
        for pp in range(pairs_per_kv):
            pair = kvh * pairs_per_kv + pp
            ge, go = 2 * pp, 2 * pp + 1
            o_even = o_all[ge * ATT_BLOCK:(ge + 1) * ATT_BLOCK] * inv[ge]
            o_odd = o_all[go * ATT_BLOCK:(go + 1) * ATT_BLOCK] * inv[go]
            o_ref[:, pair * LANE:(pair + 1) * LANE] = jnp.where(low_head, o_even, o_odd).astype(o_ref.dtype)


def _swa(proj, cos, sin, sinks, batch, seq):
    T = proj.shape[0]
    nb = seq // ATT_BLOCK
    cur = lambda b, i: b * nb + i
    prev = lambda b, i: b * nb + jnp.maximum(i - 1, 0)
    blk = lambda rows_fn, col: pl.BlockSpec((ATT_BLOCK, LANE), lambda b, i: (rows_fn(b, i), col))
    return pl.pallas_call(
        _swa_kernel,
        grid=(batch, nb),
        in_specs=[pl.BlockSpec(memory_space=pltpu.SMEM),
                  pl.BlockSpec((ATT_BLOCK, N_Q_HEADS * HEAD_DIM), lambda b, i: (cur(b, i), Q_BLK // (N_Q_HEADS * HEAD_DIM // LANE))),
                  blk(cur, K_BLK), blk(prev, K_BLK), blk(cur, V_BLK), blk(prev, V_BLK),
                  blk(cur, 0), blk(cur, 0), blk(prev, 0), blk(prev, 0)],
        out_specs=pl.BlockSpec((ATT_BLOCK, N_Q_HEADS * HEAD_DIM), lambda b, i: (cur(b, i), 0)),
        out_shape=jax.ShapeDtypeStruct((T, N_Q_HEADS * HEAD_DIM), BF16),
        compiler_params=_params(("parallel", "arbitrary"), 32),
        name="swa",
    )(sinks, proj, proj, proj, proj, proj, cos, sin, cos, sin)


def _delta_masks():
    i = np.arange(CHUNK)[:, None]
    j = (np.arange(SUPER) % CHUNK)[None, :]
    ms = [i > j, i >= j, ((i // 4) == (j // 4)) & (i > j)]
    s = 4
    while s < CHUNK:
        ms.append(((i // (2 * s)) == (j // (2 * s))) & ((i // s) % 2 == 1) & ((j // s) % 2 == 0))
        s *= 2
    ms.append(i == j)
    r = np.arange(SUPER)[:, None]
    c = np.arange(SUPER)[None, :]
    same = (r // CHUNK) == (c // CHUNK)
    return (jnp.asarray(np.stack(ms).astype(np.float32)),
            jnp.asarray((same & (r >= c)).astype(np.float32)),
            jnp.asarray(same.astype(np.float32), dtype=BF16))


def _softplus(x):
    return jnp.maximum(x, 0.0) + jnp.log(1.0 + jnp.exp(-jnp.abs(x)))


def _bdot(a, b):
    return jnp.dot(a.astype(BF16), b.astype(BF16), preferred_element_type=F32)


def _gdn_kernel(*refs):
    pieces = GDN_HPB // GDN_SPLIT
    q_refs, k_refs, v_refs, hq_refs, hk_refs, hv_refs, z_refs = (refs[i * pieces:(i + 1) * pieces] for i in range(7))
    (cwq_ref, cwk_ref, cwv_ref, tail_ref, tailt_ref, alog_r_ref, dtb_r_ref, alog_c_ref, dtb_c_ref,
     nw_ref, wmask_ref, causal_ref, same_ref, o_ref, state_ref) = refs[7 * pieces:]
    head0 = pl.program_id(1) * GDN_HPB
    step = pl.program_id(2)
    heads = range(GDN_HPB)
    sls = [slice(j * LANE, (j + 1) * LANE) for j in heads]

    @pl.when(step == 0)
    def _():
        state_ref[...] = jnp.zeros_like(state_ref)

    def head_block(piece_refs, j):
        return piece_refs[j // GDN_SPLIT][:, (j % GDN_SPLIT) * LANE:(j % GDN_SPLIT + 1) * LANE]

    def conv_silu(x_refs, halo_refs, w_ref, j):
        halo = jnp.where(step > 0, head_block(halo_refs, j), 0.0)
        xx = jnp.concatenate([halo, head_block(x_refs, j)], axis=0)
        w = w_ref[:, sls[j]]
        y = xx[8:] * w[CONV_WIDTH - 1:CONV_WIDTH]
        for j in range(CONV_WIDTH - 1):
            y = y + pltpu.roll(xx, CONV_WIDTH - 1 - j, axis=0)[8:] * w[j:j + 1]
        return y * jax.nn.sigmoid(y)

    def l2norm(x, scale):
        return x * (lax.rsqrt(jnp.sum(x * x, axis=-1, keepdims=True) + RMS_EPS) * scale)

    m_causal = causal_ref[...]
    tail = tail_ref[...]
    lane = lax.broadcasted_iota(jnp.int32, tail.shape, 1)
    g_cols = -jnp.exp(alog_r_ref[...]) * _softplus(tail + dtb_r_ref[...])
    gc_cols = jnp.dot(m_causal, g_cols, preferred_element_type=F32, precision=HIGHEST)
    tail_t = tailt_ref[...]
    sub = lax.broadcasted_iota(jnp.int32, tail_t.shape, 0)
    g_rows = -jnp.exp(alog_c_ref[...]) * _softplus(tail_t + dtb_c_ref[...])
    gc_rows = lax.dot_general(g_rows, m_causal, (((1,), (1,)), ((), ())),
                              preferred_element_type=F32, precision=HIGHEST)

    lane_r = lax.broadcasted_iota(jnp.int32, (1, SUPER), 1)
    n_chunks = SUPER // CHUNK
    chunk_of_lane = lax.broadcasted_iota(jnp.int32, (CHUNK, SUPER), 1) >> 6
    same_bd = same_ref[...]
    w_strict, w_causal, w_base, w_eye = wmask_ref[0], wmask_ref[1], wmask_ref[2], wmask_ref[3 + 4]

    def diag_blocks_wide(full):
        out = full[(n_chunks - 1) * CHUNK:]
        for c in range(n_chunks - 2, -1, -1):
            out = jnp.where(chunk_of_lane == c, full[c * CHUNK:(c + 1) * CHUNK], out)
        return out

    def column_wide(col):
        out = jnp.broadcast_to(col[(n_chunks - 1) * CHUNK:], (CHUNK, SUPER))
        for c in range(n_chunks - 2, -1, -1):
            out = jnp.where(chunk_of_lane == c, jnp.broadcast_to(col[c * CHUNK:(c + 1) * CHUNK], (CHUNK, SUPER)), out)
        return out

    def block_diag(wide):
        return jnp.concatenate([wide.astype(BF16)] * n_chunks, axis=0) * same_bd

    def wdot(a_wide, b_wide):
        return jnp.dot(a_wide.astype(BF16), block_diag(b_wide), preferred_element_type=F32)

    q = [l2norm(conv_silu(q_refs, hq_refs, cwq_ref, j), DN_HEAD_DIM ** -0.5) for j in heads]
    k = [l2norm(conv_silu(k_refs, hk_refs, cwk_ref, j), 1.0) for j in heads]
    v = [conv_silu(v_refs, hv_refs, cwv_ref, j) for j in heads]
    gc_col = [jnp.sum(jnp.where(lane == head0 + j, gc_cols, 0.0), axis=-1, keepdims=True) for j in heads]
    beta = [jax.nn.sigmoid(jnp.sum(jnp.where(lane == DN_HEADS + head0 + j, tail, 0.0), axis=-1, keepdims=True))
            for j in heads]
    gc_row = [jnp.sum(jnp.where(sub == head0 + j, gc_rows, 0.0), axis=0, keepdims=True) for j in heads]
    decay = [jnp.exp(jnp.where(w_causal > 0, column_wide(gc_col[j]) - gc_row[j], NEG)) for j in heads]
    kb = [k[j] * beta[j] for j in heads]
    vb = [v[j] * beta[j] for j in heads]
    kq = [lax.dot_general(jnp.concatenate([kb[j], q[j]], axis=0).astype(BF16), k[j].astype(BF16),
                          (((1,), (1,)), ((), ())), preferred_element_type=F32) for j in heads]
    lmat = [diag_blocks_wide(kq[j][:SUPER]) * decay[j] * w_strict for j in heads]
    a_qk = [diag_blocks_wide(kq[j][SUPER:]) * decay[j] for j in heads]

    l4 = [lmat[j] * w_base for j in heads]
    l4sq = [wdot(l4[j], l4[j]) for j in heads]
    l4cu = [wdot(l4[j], l4sq[j]) for j in heads]
    tinv = [w_eye - l4[j] + l4sq[j] - l4cu[j] for j in heads]
    for lvl in range(4):
        ct = [wdot(lmat[j] * wmask_ref[3 + lvl], tinv[j]) for j in heads]
        tinv = [tinv[j] - wdot(tinv[j], ct[j]) for j in heads]

    e_col = [jnp.exp(gc_col[j]) for j in heads]
    uw = [jnp.dot(block_diag(tinv[j]), jnp.concatenate([vb[j], kb[j] * e_col[j]], axis=1).astype(BF16),
                  preferred_element_type=F32) for j in heads]
    qg = [q[j] * e_col[j] for j in heads]
    kd_t = []
    for j in heads:
        gc_last_row = jnp.zeros((1, SUPER), F32)
        for c in range(SUPER // CHUNK):
            last = gc_row[j][:, (c + 1) * CHUNK - 1:(c + 1) * CHUNK]
            gc_last_row = jnp.where((lane_r >> 6) == c, last, gc_last_row)
        kd_t.append(k[j].T * jnp.exp(gc_last_row - gc_row[j]))

    ak = [jnp.concatenate([a_qk[j], kd_t[j]], axis=0).astype(BF16) for j in heads]
    state = [state_ref[j] for j in heads]
    outs = [[] for _ in heads]
    for c in range(n_chunks):
        r0 = c * CHUNK
        rows = slice(r0, r0 + CHUNK)
        ws = [_bdot(jnp.concatenate([uw[j][rows, LANE:], qg[j][rows]], axis=0), state[j]) for j in heads]
        v_new = [uw[j][rows, :LANE] - ws[j][:CHUNK] for j in heads]
        v_pad = [jnp.concatenate(([jnp.zeros((r0, LANE), BF16)] if r0 else []) + [v_new[j].astype(BF16)]
                                 + ([jnp.zeros((SUPER - r0 - CHUNK, LANE), BF16)] if r0 + CHUNK < SUPER else []),
                                 axis=0) for j in heads]
        both = [jnp.dot(ak[j], v_pad[j], preferred_element_type=F32) for j in heads]
        for j in heads:
            outs[j].append(ws[j][CHUNK:] + both[j][:CHUNK])
        state = [state[j] * jnp.exp(gc_row[j][:, r0 + CHUNK - 1:r0 + CHUNK]) + both[j][CHUNK:] for j in heads]
    for j in heads:
        state_ref[j] = state[j]
        o = jnp.concatenate(outs[j], axis=0)
        z = head_block(z_refs, j)
        o = o * lax.rsqrt(jnp.mean(o * o, axis=-1, keepdims=True) + RMS_EPS) * nw_ref[...] * (z * jax.nn.sigmoid(z))
        o_ref[:, sls[j]] = o.astype(o_ref.dtype)


def _gdn(proj, tail, tail_t, conv_w, alog_r, dtb_r, alog_c, dtb_c, norm_w, masks, batch, seq):
    T = proj.shape[0]
    ns = seq // SUPER
    hpb = GDN_HPB
    halo_per_super = SUPER // 8
    row = lambda b, i: b * ns + i
    halo_row = lambda b, i: jnp.maximum((b * ns + i) * halo_per_super - 1, 0)
    pieces = hpb // GDN_SPLIT
    piece_col = lambda col, h, m: col // GDN_SPLIT + h * pieces + m
    cur = lambda col: [pl.BlockSpec((SUPER, GDN_SPLIT * LANE), lambda b, h, i, m=m: (row(b, i), piece_col(col, h, m)))
                       for m in range(pieces)]
    halo = lambda col: [pl.BlockSpec((8, GDN_SPLIT * LANE), lambda b, h, i, m=m: (halo_row(b, i), piece_col(col, h, m)))
                        for m in range(pieces)]
    cw = lambda col: pl.BlockSpec((CONV_WIDTH, hpb * LANE), lambda b, h, i: (0, col // hpb + h))
    const2 = lambda shape: pl.BlockSpec(shape, lambda b, h, i: (0, 0))
    return pl.pallas_call(
        _gdn_kernel,
        grid=(batch, DN_HEADS // hpb, ns),
        in_specs=[*cur(DNQ_BLK), *cur(DNK_BLK), *cur(DNV_BLK),
                  *halo(DNQ_BLK), *halo(DNK_BLK), *halo(DNV_BLK), *cur(Z_BLK),
                  cw(0), cw(DN_HEADS), cw(2 * DN_HEADS),
                  pl.BlockSpec((SUPER, LANE), lambda b, h, i: (row(b, i), 0)),
                  pl.BlockSpec((16, SUPER), lambda b, h, i: (0, row(b, i))),
                  const2((1, LANE)), const2((1, LANE)), const2((16, 1)), const2((16, 1)),
                  const2((1, LANE)),
                  pl.BlockSpec(masks[0].shape, lambda b, h, i: (0, 0, 0)),
                  const2((SUPER, SUPER)), const2((SUPER, SUPER))],
        out_specs=pl.BlockSpec((SUPER, hpb * LANE), lambda b, h, i: (row(b, i), h)),
        out_shape=jax.ShapeDtypeStruct((T, DN_HEADS * DN_HEAD_DIM), BF16),
        scratch_shapes=[pltpu.VMEM((hpb, DN_HEAD_DIM, DN_HEAD_DIM), F32)],
        compiler_params=_params(("parallel", "parallel", "arbitrary"), 40),
        name="gdn",
    )(*([proj] * (7 * pieces)), conv_w, conv_w, conv_w, tail, tail_t,
      alog_r, dtb_r, alog_c, dtb_c, norm_w, *masks)


def _layer_norm(y, g, b):
    mu = jnp.mean(y, axis=-1, keepdims=True)
    yc = y - mu
    var = jnp.mean(yc * yc, axis=-1, keepdims=True)
    return yc * lax.rsqrt(var + LN_EPS) * g + b


def _outproj_kernel(alpha, a_ref, d_ref, wa_ref, wd_ref, x_ref, g_ref, b_ref, o_ref, ob_ref):
    mix = jnp.dot(a_ref[...], wa_ref[...], preferred_element_type=F32)
    mix = mix + jnp.dot(d_ref[...], wd_ref[...], preferred_element_type=F32)
    out = _layer_norm(alpha * x_ref[...] + mix, g_ref[...], b_ref[...])
    o_ref[...] = out
    ob_ref[...] = out.astype(BF16)


def _outproj_ln(attn, dn, w_out, x2d, g, b, l, alpha):
    T, D = x2d.shape
    tm = 256
    half = attn.shape[1]
    return pl.pallas_call(
        functools.partial(_outproj_kernel, alpha),
        grid=(T // tm,),
        in_specs=[pl.BlockSpec((tm, half), lambda i: (i, 0)),
                  pl.BlockSpec((tm, half), lambda i: (i, 0)),
                  pl.BlockSpec((None, half, D), lambda i: (l, 0, 0)),
                  pl.BlockSpec((None, half, D), lambda i: (l, 1, 0)),
                  pl.BlockSpec((tm, D), lambda i: (i, 0)),
                  pl.BlockSpec((None, 1, D), lambda i: (l, 0, 0)),
                  pl.BlockSpec((None, 1, D), lambda i: (l, 0, 0))],
        out_specs=[pl.BlockSpec((tm, D), lambda i: (i, 0))] * 2,
        out_shape=[jax.ShapeDtypeStruct((T, D), F32), jax.ShapeDtypeStruct((T, D), BF16)],
        compiler_params=_params(("parallel",), 48),
        name="outproj_ln",
    )(attn, dn, w_out, w_out, x2d, g, b)


def _router_kernel(x_ref, w_ref, bias_ref, gates_ref, grp_ref):
    xh, xl = _split_bf16(x_ref[...])
    w_parts = jnp.concatenate(_split_bf16(w_ref[...]), axis=1)
    both = jnp.dot(xh, w_parts, preferred_element_type=F32) + jnp.dot(xl, w_parts, preferred_element_type=F32)
    logits = both[:, :N_EXPERTS] + both[:, N_EXPERTS:]
    scores = jax.nn.sigmoid(logits)
    sel = scores + bias_ref[...]
    e = lax.broadcasted_iota(jnp.int32, sel.shape, 1)
    big = jnp.int32(N_EXPERTS)

    def top2(vals):
        m1 = jnp.max(vals, axis=-1, keepdims=True)
        i1 = jnp.min(jnp.where(vals == m1, e, big), axis=-1, keepdims=True)
        rest = jnp.where(e == i1, -jnp.inf, vals)
        m2 = jnp.max(rest, axis=-1, keepdims=True)
        i2 = jnp.min(jnp.where(rest == m2, e, big), axis=-1, keepdims=True)
        return m1, i1, m2, i2

    best = None
    for grp in range(N_GROUPS):
        m1, i1, m2, i2 = top2(jnp.where((e >> 2) == grp, sel, -jnp.inf))
        gsum = m1 + m2
        if best is None:
            best = (gsum, i1, i2)
        else:
            take = gsum > best[0]
            best = (jnp.where(take, gsum, best[0]), jnp.where(take, i1, best[1]), jnp.where(take, i2, best[2]))
    _, e1, e2 = best
    w1 = jnp.sum(jnp.where(e == e1, scores, 0.0), axis=-1, keepdims=True)
    w2 = jnp.sum(jnp.where(e == e2, scores, 0.0), axis=-1, keepdims=True)
    tot = w1 + w2
    gates_ref[...] = jnp.where(e == e1, w1 / tot, 0.0) + jnp.where(e == e2, w2 / tot, 0.0)
    grp_ref[...] = e1 >> 2


def _router(x2d, router_w, router_bias):
    T, D = x2d.shape
    tm = 512
    return pl.pallas_call(
        _router_kernel,
        grid=(T // tm,),
        in_specs=[pl.BlockSpec((tm, D), lambda i: (i, 0)),
                  pl.BlockSpec((D, N_EXPERTS), lambda i: (0, 0)),
                  pl.BlockSpec((1, N_EXPERTS), lambda i: (0, 0))],
        out_specs=[pl.BlockSpec((tm, N_EXPERTS), lambda i: (i, 0)), pl.BlockSpec((tm, 1), lambda i: (i, 0))],
        out_shape=[jax.ShapeDtypeStruct((T, N_EXPERTS), F32), jax.ShapeDtypeStruct((T, 1), jnp.int32)],
        compiler_params=_params(("parallel",), 32),
        name="router",
    )(x2d, router_w, router_bias.reshape(1, N_EXPERTS))


def _one_hot_rows(idx_col, width):
    lane = lax.broadcasted_iota(jnp.int32, (idx_col.shape[0], width), 1)
    return jnp.where(lane == idx_col, 1.0, 0.0).astype(BF16)


def _dispatch_kernel(idx_ref, xb_ref, gates_ref, xs_ref, gs_ref):
    p = _one_hot_rows(idx_ref[...], MOE_WINDOW)
    xs_ref[...] = jnp.dot(p, xb_ref[...], preferred_element_type=F32).astype(BF16)
    g_parts = jnp.concatenate(_split_bf16(gates_ref[...]), axis=1)
    both = jnp.dot(p, g_parts, preferred_element_type=F32)
    gs_ref[...] = both[:, :N_EXPERTS] + both[:, N_EXPERTS:]


def _dispatch(order_col, xb, gates):
    T, D = xb.shape
    tr = MOE_TM
    per = MOE_WINDOW // tr
    return pl.pallas_call(
        _dispatch_kernel,
        grid=(T // MOE_WINDOW, per),
        in_specs=[pl.BlockSpec((tr, 1), lambda w, j: (w * per + j, 0)),
                  pl.BlockSpec((MOE_WINDOW, D), lambda w, j: (w, 0)),
                  pl.BlockSpec((MOE_WINDOW, N_EXPERTS), lambda w, j: (w, 0))],
        out_specs=[pl.BlockSpec((tr, D), lambda w, j: (w * per + j, 0)),
                   pl.BlockSpec((tr, N_EXPERTS), lambda w, j: (w * per + j, 0))],
        out_shape=[jax.ShapeDtypeStruct((T, D), BF16), jax.ShapeDtypeStruct((T, N_EXPERTS), F32)],
        compiler_params=_params(("parallel", "arbitrary"), 48),
        name="moe_dispatch",
    )(order_col, xb, gates)


def _moe_kernel(act_ref, fetch_ref, x_ref, gates_ref, wg_ref, wu_ref, wd_ref, o_ref, acc_ref):
    tile, step = pl.program_id(0), pl.program_id(1)
    n_steps = N_EXPERTS // MOE_EPS

    @pl.when(step == 0)
    def _():
        acc_ref[...] = jnp.zeros_like(acc_ref)

    @pl.when(act_ref[tile * n_steps + step] > 0)
    def _():
        xb = x_ref[...]
        gates = gates_ref[...]
        lane = lax.broadcasted_iota(jnp.int32, gates.shape, 1)
        out = None
        for k in range(MOE_EPS):
            a = jnp.dot(xb, wg_ref[k], preferred_element_type=F32)
            u = jnp.dot(xb, wu_ref[k], preferred_element_type=F32)
            gate = jnp.sum(jnp.where(lane == step * MOE_EPS + k, gates, 0.0), axis=-1, keepdims=True)
            hidden = (a * jax.nn.sigmoid(a)) * u * gate
            part = jnp.dot(hidden.astype(BF16), wd_ref[k], preferred_element_type=F32)
            out = part if out is None else out + part
        acc_ref[...] += out

    @pl.when(step == n_steps - 1)
    def _():
        o_ref[...] = acc_ref[...].astype(o_ref.dtype)


def _moe_ffn(active, fetch, xs, gates_s, w_gate, w_up, w_down, l):
    T, D = xs.shape
    tm = MOE_TM
    n_steps = N_EXPERTS // MOE_EPS
    wmap = lambda i, s, act, fetch: (l, fetch[i * n_steps + s], 0, 0)
    return pl.pallas_call(
        _moe_kernel,
        grid_spec=pltpu.PrefetchScalarGridSpec(
            num_scalar_prefetch=2,
            grid=(T // tm, n_steps),
            in_specs=[pl.BlockSpec((tm, D), lambda i, e, act, fetch: (i, 0)),
                      pl.BlockSpec((tm, N_EXPERTS), lambda i, e, act, fetch: (i, 0)),
                      pl.BlockSpec((None, MOE_EPS, D, D_EXPERT), wmap),
                      pl.BlockSpec((None, MOE_EPS, D, D_EXPERT), wmap),
                      pl.BlockSpec((None, MOE_EPS, D_EXPERT, D), wmap)],
            out_specs=pl.BlockSpec((tm, D), lambda i, e, act, fetch: (i, 0)),
            scratch_shapes=[pltpu.VMEM((tm, D), F32)]),
        out_shape=jax.ShapeDtypeStruct((T, D), BF16),
        compiler_params=_params(("parallel", "arbitrary"), 48),
        name="moe_ffn",
    )(active, fetch, xs, gates_s, w_gate, w_up, w_down)


def _combine_kernel(alpha, idx_ref, f_ref, x_ref, g_ref, b_ref, o_ref):
    p = _one_hot_rows(idx_ref[...], MOE_WINDOW)
    ffn = jnp.dot(p, f_ref[...], preferred_element_type=F32)
    o_ref[...] = _layer_norm(alpha * x_ref[...] + ffn, g_ref[...], b_ref[...])


def _combine_ln(dest_col, ffn_s, x2d, g, b, l, alpha):
    T, D = x2d.shape
    tr = MOE_TM
    per = MOE_WINDOW // tr
    return pl.pallas_call(
        functools.partial(_combine_kernel, alpha),
        grid=(T // MOE_WINDOW, per),
        in_specs=[pl.BlockSpec((tr, 1), lambda w, j: (w * per + j, 0)),
                  pl.BlockSpec((MOE_WINDOW, D), lambda w, j: (w, 0)),
                  pl.BlockSpec((tr, D), lambda w, j: (w * per + j, 0)),
                  pl.BlockSpec((None, 1, D), lambda w, j: (l, 0, 0)),
                  pl.BlockSpec((None, 1, D), lambda w, j: (l, 0, 0))],
        out_specs=pl.BlockSpec((tr, D), lambda w, j: (w * per + j, 0)),
        out_shape=jax.ShapeDtypeStruct((T, D), F32),
        compiler_params=_params(("parallel", "arbitrary"), 48),
        name="moe_combine_ln",
    )(dest_col, ffn_s, x2d, g, b)


def _dispatch_plan(grp, T):
    nw = T // MOE_WINDOW
    g2 = grp.reshape(nw, MOE_WINDOW)
    order = jnp.argsort(g2, axis=1, stable=True).astype(jnp.int32)
    dest = jnp.argsort(order, axis=1).astype(jnp.int32)
    grp_sorted = jnp.take_along_axis(g2, order, axis=1).reshape(T // MOE_TM, MOE_TM)
    present = (grp_sorted[:, :, None] == jnp.arange(N_GROUPS, dtype=jnp.int32)[None, None, :]).any(axis=1)
    steps_per_group = EXPERTS_PER_GROUP // MOE_EPS
    active = jnp.repeat(present, steps_per_group, axis=1).astype(jnp.int32).reshape(-1)
    steps = jnp.arange(active.size, dtype=jnp.int32)
    last_active = lax.cummax(jnp.where(active > 0, steps, -1))
    fetch = jnp.where(last_active >= 0, last_active % (N_EXPERTS // MOE_EPS), 0).astype(jnp.int32)
    return order.reshape(T, 1), dest.reshape(T, 1), active, fetch


def kernel(x, positions, w_in, attn_sinks, conv_w, a_log, dt_bias, dn_norm_w, w_out, ln1_g, ln1_b,
           router_w, router_bias, w_gate, w_up, w_down, ln2_g, ln2_b):
    batch, seq, d = x.shape
    depth = w_in.shape[0]
    T = batch * seq
    alpha = float((2 * depth) ** 0.25)

    w_main = w_in.astype(BF16)
    w_tail = jnp.concatenate(_split_bf16(
        jnp.pad(w_in[:, :, MAIN_COLS:], ((0, 0), (0, 0), (0, LANE - 2 * DN_HEADS)))), axis=-1)
    w_out_b = w_out.astype(BF16)
    w_gate_b, w_up_b, w_down_b = w_gate.astype(BF16), w_up.astype(BF16), w_down.astype(BF16)
    pad_heads = lambda t, n: jnp.pad(t, ((0, 0), (0, n - DN_HEADS)))
    alog_r = pad_heads(a_log, LANE).reshape(depth, 1, LANE)
    dtb_r = pad_heads(dt_bias, LANE).reshape(depth, 1, LANE)
    alog_c = pad_heads(a_log, 16).reshape(depth, 16, 1)
    dtb_c = pad_heads(dt_bias, 16).reshape(depth, 16, 1)
    masks = _delta_masks()
    ln1_g3, ln1_b3 = ln1_g.reshape(depth, 1, d), ln1_b.reshape(depth, 1, d)
    ln2_g3, ln2_b3 = ln2_g.reshape(depth, 1, d), ln2_b.reshape(depth, 1, d)

    cos, sin = _rope_tables(positions)
    h = x.reshape(T, d)
    for l in range(depth):
        proj, tail = _inproj(h, w_main, w_tail, l)
        tail_t = tail[:, :2 * DN_HEADS].T
        attn = _swa(proj, cos, sin, attn_sinks[l], batch, seq)
        dn = _gdn(proj, tail, tail_t, conv_w[l], alog_r[l], dtb_r[l], alog_c[l], dtb_c[l],
                  dn_norm_w[l].reshape(1, DN_HEAD_DIM), masks, batch, seq)
        h, hb = _outproj_ln(attn, dn, w_out_b, h, ln1_g3, ln1_b3, l, alpha)
        gates, grp = _router(h, router_w, router_bias)
        order_col, dest_col, active, fetch = _dispatch_plan(grp, T)
        xs, gates_s = _dispatch(order_col, hb, gates)
        ffn_s = _moe_ffn(active, fetch, xs, gates_s, w_gate_b, w_up_b, w_down_b, l)
        h = _combine_ln(dest_col, ffn_s, h, ln2_g3, ln2_b3, l, alpha)
    return h.reshape(batch, seq, d)
```

```python
import functools

import numpy as np
import jax
import jax.numpy as jnp
from jax import lax
from jax.experimental import pallas as pl
from jax.experimental.pallas import tpu as pltpu

F32 = jnp.float32
BF16 = jnp.bfloat16
HIGHEST = lax.Precision.HIGHEST

D_MODEL = 2048
HEAD_DIM = 64
N_Q_HEADS = 16
KV_GROUP = 8
N_KV_HEADS = 2
ATT_BLOCK = 128
ROPE_THETA = 10000.0
DN_HEADS = 8
DN_HEAD_DIM = 128
CONV_WIDTH = 4
CHUNK = 64
SUPER = 256
N_EXPERTS = 16
N_GROUPS = 4
EXPERTS_PER_GROUP = 4
D_EXPERT = 512
LN_EPS = 1e-5
RMS_EPS = 1e-6
NEG = -1e30

LANE = 128
Q_BLK, K_BLK, V_BLK = 0, 8, 9
DNQ_BLK, DNK_BLK, DNV_BLK, Z_BLK = 10, 18, 26, 34
MAIN_COLS = 42 * LANE
GDN_HPB = 8
MOE_WINDOW = 2048
MOE_TM = 512
MOE_EPS = 2
GDN_SPLIT = 2
MIB = 1024 * 1024


def _params(sem, vmem_mib):
    return pltpu.CompilerParams(dimension_semantics=sem, vmem_limit_bytes=vmem_mib * MIB)


def _split_bf16(t):
    hi = t.astype(BF16)
    return hi, (t - hi.astype(F32)).astype(BF16)


def _inproj_kernel(x_ref, w_ref, wt_ref, o_ref, t_ref, xh_ref):
    @pl.when(pl.program_id(1) == 0)
    def _():
        xh, xl = _split_bf16(x_ref[...])
        xh_ref[...] = xh
        wt = wt_ref[...]
        both = jnp.dot(xh, wt, preferred_element_type=F32) + jnp.dot(xl, wt, preferred_element_type=F32)
        t_ref[...] = both[:, :LANE] + both[:, LANE:]

    o_ref[...] = jnp.dot(xh_ref[...], w_ref[...], preferred_element_type=F32)


def _inproj(x2d, w_main, w_tail, l):
    T, K = x2d.shape
    tm, tn = min(1024, T), 768
    return pl.pallas_call(
        _inproj_kernel,
        grid=(T // tm, MAIN_COLS // tn),
        in_specs=[pl.BlockSpec((tm, K), lambda i, j: (i, 0)),
                  pl.BlockSpec((None, K, tn), lambda i, j: (l, 0, j)),
                  pl.BlockSpec((None, K, 2 * LANE), lambda i, j: (l, 0, 0))],
        out_specs=[pl.BlockSpec((tm, tn), lambda i, j: (i, j)),
                   pl.BlockSpec((tm, LANE), lambda i, j: (i, 0))],
        out_shape=[jax.ShapeDtypeStruct((T, MAIN_COLS), F32),
                   jax.ShapeDtypeStruct((T, LANE), F32)],
        scratch_shapes=[pltpu.VMEM((tm, K), BF16)],
        compiler_params=_params(("parallel", "arbitrary"), 40),
        name="inproj",
    )(x2d, w_main, w_tail)


def _rope_kernel(pos_ref, inv_ref, cos_ref, sin_ref):
    ang = pos_ref[...].astype(F32) * inv_ref[...]
    lane = lax.broadcasted_iota(jnp.int32, ang.shape, 1)
    sign = jnp.where((lane & (HEAD_DIM - 1)) < HEAD_DIM // 2, -1.0, 1.0)
    cos_ref[...] = jnp.cos(ang)
    sin_ref[...] = jnp.sin(ang) * sign


def _rope_tables(positions):
    T = positions.size
    tm = min(1024, T)
    half = HEAD_DIM // 2
    inv_freq = 1.0 / (ROPE_THETA ** (jnp.arange(0, HEAD_DIM, 2, dtype=F32) / HEAD_DIM))
    inv = jnp.tile(inv_freq, LANE // half).reshape(1, LANE)
    return pl.pallas_call(
        _rope_kernel,
        grid=(T // tm,),
        in_specs=[pl.BlockSpec((tm, 1), lambda i: (i, 0)),
                  pl.BlockSpec((1, LANE), lambda i: (0, 0))],
        out_specs=[pl.BlockSpec((tm, LANE), lambda i: (i, 0))] * 2,
        out_shape=[jax.ShapeDtypeStruct((T, LANE), F32)] * 2,
        compiler_params=_params(("parallel",), 32),
        name="rope_tables",
    )(positions.reshape(T, 1), inv)


def _rope(x, cos, sin):
    lane = lax.broadcasted_iota(jnp.int32, x.shape, 1)
    first_half = (lane & (HEAD_DIM - 1)) < HEAD_DIM // 2
    swapped = jnp.where(first_half,
                        pltpu.roll(x, LANE - HEAD_DIM // 2, axis=1),
                        pltpu.roll(x, HEAD_DIM // 2, axis=1))
    return x * cos + swapped * sin


def _swa_kernel(sink_ref, q_ref, kc_ref, kp_ref, vc_ref, vp_ref, cc_ref, sc_ref, cp_ref, sp_ref, o_ref):
    blk = pl.program_id(1)
    cos_c, sin_c = cc_ref[...], sc_ref[...]
    k_cur = _rope(kc_ref[...], cos_c, sin_c)
    k_prev = _rope(kp_ref[...], cp_ref[...], sp_ref[...])
    k_cat = jnp.concatenate([k_prev, k_cur], axis=0).astype(BF16)
    v_cat = jnp.concatenate([vp_ref[...], vc_ref[...]], axis=0).astype(BF16)

    qi = lax.broadcasted_iota(jnp.int32, (ATT_BLOCK, 2 * ATT_BLOCK), 0)
    kj = lax.broadcasted_iota(jnp.int32, (ATT_BLOCK, 2 * ATT_BLOCK), 1)
    first_key = jnp.where(blk > 0, 0, ATT_BLOCK)
    valid = (kj > qi) & (kj <= qi + ATT_BLOCK) & (kj >= first_key)

    lane = lax.broadcasted_iota(jnp.int32, (ATT_BLOCK, LANE), 1)
    low_head = lane < HEAD_DIM
    pairs_per_kv = KV_GROUP // 2
    group = range(KV_GROUP)
    for kvh in range(N_KV_HEADS):
        k_h = k_cat[:, kvh * HEAD_DIM:(kvh + 1) * HEAD_DIM]
        v_h = v_cat[:, kvh * HEAD_DIM:(kvh + 1) * HEAD_DIM]
        k_dup = jnp.concatenate([k_h, k_h], axis=1)
        v_dup = jnp.concatenate([v_h, v_h], axis=1)
        q_rows = []
        for pp in range(pairs_per_kv):
            pair = kvh * pairs_per_kv + pp
            qp = _rope(q_ref[:, pair * LANE:(pair + 1) * LANE], cos_c, sin_c) * (HEAD_DIM ** -0.5)
            q_rows += [jnp.where(low_head, qp, 0.0), jnp.where(low_head, 0.0, qp)]
        q_stack = jnp.concatenate(q_rows, axis=0).astype(BF16)
        s_all = lax.dot_general(q_stack, k_dup, (((1,), (1,)), ((), ())), preferred_element_type=F32)
        sinks = [sink_ref[kvh * KV_GROUP + g] for g in group]
        s = [jnp.where(valid, s_all[g * ATT_BLOCK:(g + 1) * ATT_BLOCK], NEG) for g in group]
        m = [jnp.maximum(jnp.max(s[g], axis=-1, keepdims=True), sinks[g]) for g in group]
        p = [jnp.exp(s[g] - m[g]) for g in group]
        inv = [1.0 / (jnp.sum(p[g], axis=-1, keepdims=True) + jnp.exp(sinks[g] - m[g])) for g in group]
        p_stack = jnp.concatenate([p[g].astype(BF16) for g in group], axis=0)
        o_all = jnp.dot(p_stack, v_dup, preferred_element_type=F32)
        for pp in range(pairs_per_kv):
            pair = kvh * pairs_per_kv + pp
            ge, go = 2 * pp, 2 * pp + 1
            o_even = o_all[ge * ATT_BLOCK:(ge + 1) * ATT_BLOCK] * inv[ge]
            o_odd = o_all[go * ATT_BLOCK:(go + 1) * ATT_BLOCK] * inv[go]
            o_ref[:, pair * LANE:(pair + 1) * LANE] = jnp.where(low_head, o_even, o_odd).astype(o_ref.dtype)


def _swa(proj, cos, sin, sinks, batch, seq):
    T = proj.shape[0]
    nb = seq // ATT_BLOCK
    cur = lambda b, i: b * nb + i
    prev = lambda b, i: b * nb + jnp.maximum(i - 1, 0)
    blk = lambda rows_fn, col: pl.BlockSpec((ATT_BLOCK, LANE), lambda b, i: (rows_fn(b, i), col))
    return pl.pallas_call(
        _swa_kernel,
        grid=(batch, nb),
        in_specs=[pl.BlockSpec(memory_space=pltpu.SMEM),
                  pl.BlockSpec((ATT_BLOCK, N_Q_HEADS * HEAD_DIM), lambda b, i: (cur(b, i), Q_BLK // (N_Q_HEADS * HEAD_DIM // LANE))),
                  blk(cur, K_BLK), blk(prev, K_BLK), blk(cur, V_BLK), blk(prev, V_BLK),
                  blk(cur, 0), blk(cur, 0), blk(prev, 0), blk(prev, 0)],
        out_specs=pl.BlockSpec((ATT_BLOCK, N_Q_HEADS * HEAD_DIM), lambda b, i: (cur(b, i), 0)),
        out_shape=jax.ShapeDtypeStruct((T, N_Q_HEADS * HEAD_DIM), BF16),
        compiler_params=_params(("parallel", "arbitrary"), 32),
        name="swa",
    )(sinks, proj, proj, proj, proj, proj, cos, sin, cos, sin)


def _delta_masks():
    i = np.arange(CHUNK)[:, None]
    j = (np.arange(SUPER) % CHUNK)[None, :]
    ms = [i > j, i >= j, ((i // 4) == (j // 4)) & (i > j)]
    s = 4
    while s < CHUNK:
        ms.append(((i // (2 * s)) == (j // (2 * s))) & ((i // s) % 2 == 1) & ((j // s) % 2 == 0))
        s *= 2
    ms.append(i == j)
    r = np.arange(SUPER)[:, None]
    c = np.arange(SUPER)[None, :]
    same = (r // CHUNK) == (c // CHUNK)
    return (jnp.asarray(np.stack(ms).astype(np.float32)),
            jnp.asarray((same & (r >= c)).astype(np.float32)),
            jnp.asarray(same.astype(np.float32), dtype=BF16))


def _softplus(x):
    return jnp.maximum(x, 0.0) + jnp.log(1.0 + jnp.exp(-jnp.abs(x)))


def _bdot(a, b):
    return jnp.dot(a.astype(BF16), b.astype(BF16), preferred_element_type=F32)


def _gdn_kernel(*refs):
    pieces = GDN_HPB // GDN_SPLIT
    q_refs, k_refs, v_refs, hq_refs, hk_refs, hv_refs, z_refs = (refs[i * pieces:(i + 1) * pieces] for i in range(7))
    (cwq_ref, cwk_ref, cwv_ref, tail_ref, tailt_ref, alog_r_ref, dtb_r_ref, alog_c_ref, dtb_c_ref,
     nw_ref, wmask_ref, causal_ref, same_ref, o_ref, state_ref) = refs[7 * pieces:]
    head0 = pl.program_id(1) * GDN_HPB
    step = pl.program_id(2)
    heads = range(GDN_HPB)
    sls = [slice(j * LANE, (j + 1) * LANE) for j in heads]

    @pl.when(step == 0)
    def _():
        state_ref[...] = jnp.zeros_like(state_ref)

    def head_block(piece_refs, j):
        return piece_refs[j // GDN_SPLIT][:, (j % GDN_SPLIT) * LANE:(j % GDN_SPLIT + 1) * LANE]

    def conv_silu(x_refs, halo_refs, w_ref, j):
        halo = jnp.where(step > 0, head_block(halo_refs, j), 0.0)
        xx = jnp.concatenate([halo, head_block(x_refs, j)], axis=0)
        w = w_ref[:, sls[j]]
        y = xx[8:] * w[CONV_WIDTH - 1:CONV_WIDTH]
        for j in range(CONV_WIDTH - 1):
            y = y + pltpu.roll(xx, CONV_WIDTH - 1 - j, axis=0)[8:] * w[j:j + 1]
        return y * jax.nn.sigmoid(y)

    def l2norm(x, scale):
        return x * (lax.rsqrt(jnp.sum(x * x, axis=-1, keepdims=True) + RMS_EPS) * scale)

    m_causal = causal_ref[...]
    tail = tail_ref[...]
    lane = lax.broadcasted_iota(jnp.int32, tail.shape, 1)
    g_cols = -jnp.exp(alog_r_ref[...]) * _softplus(tail + dtb_r_ref[...])
    gc_cols = jnp.dot(m_causal, g_cols, preferred_element_type=F32, precision=HIGHEST)
    tail_t = tailt_ref[...]
    sub = lax.broadcasted_iota(jnp.int32, tail_t.shape, 0)
    g_rows = -jnp.exp(alog_c_ref[...]) * _softplus(tail_t + dtb_c_ref[...])
    gc_rows = lax.dot_general(g_rows, m_causal, (((1,), (1,)), ((), ())),
                              preferred_element_type=F32, precision=HIGHEST)

    lane_r = lax.broadcasted_iota(jnp.int32, (1, SUPER), 1)
    n_chunks = SUPER // CHUNK
    chunk_of_lane = lax.broadcasted_iota(jnp.int32, (CHUNK, SUPER), 1) >> 6
    same_bd = same_ref[...]
    w_strict, w_causal, w_base, w_eye = wmask_ref[0], wmask_ref[1], wmask_ref[2], wmask_ref[3 + 4]

    def diag_blocks_wide(full):
        out = full[(n_chunks - 1) * CHUNK:]
        for c in range(n_chunks - 2, -1, -1):
            out = jnp.where(chunk_of_lane == c, full[c * CHUNK:(c + 1) * CHUNK], out)
        return out

    def column_wide(col):
        out = jnp.broadcast_to(col[(n_chunks - 1) * CHUNK:], (CHUNK, SUPER))
        for c in range(n_chunks - 2, -1, -1):
            out = jnp.where(chunk_of_lane == c, jnp.broadcast_to(col[c * CHUNK:(c + 1) * CHUNK], (CHUNK, SUPER)), out)
        return out

    def block_diag(wide):
        return jnp.concatenate([wide.astype(BF16)] * n_chunks, axis=0) * same_bd

    def wdot(a_wide, b_wide):
        return jnp.dot(a_wide.astype(BF16), block_diag(b_wide), preferred_element_type=F32)

    q = [l2norm(conv_silu(q_refs, hq_refs, cwq_ref, j), DN_HEAD_DIM ** -0.5) for j in heads]
    k = [l2norm(conv_silu(k_refs, hk_refs, cwk_ref, j), 1.0) for j in heads]
    v = [conv_silu(v_refs, hv_refs, cwv_ref, j) for j in heads]
    gc_col = [jnp.sum(jnp.where(lane == head0 + j, gc_cols, 0.0), axis=-1, keepdims=True) for j in heads]
    beta = [jax.nn.sigmoid(jnp.sum(jnp.where(lane == DN_HEADS + head0 + j, tail, 0.0), axis=-1, keepdims=True))
            for j in heads]
    gc_row = [jnp.sum(jnp.where(sub == head0 + j, gc_rows, 0.0), axis=0, keepdims=True) for j in heads]
    decay = [jnp.exp(jnp.where(w_causal > 0, column_wide(gc_col[j]) - gc_row[j], NEG)) for j in heads]
    kb = [k[j] * beta[j] for j in heads]
    vb = [v[j] * beta[j] for j in heads]
    kq = [lax.dot_general(jnp.concatenate([kb[j], q[j]], axis=0).astype(BF16), k[j].astype(BF16),
                          (((1,), (1,)), ((), ())), preferred_element_type=F32) for j in heads]
    lmat = [diag_blocks_wide(kq[j][:SUPER]) * decay[j] * w_strict for j in heads]
    a_qk = [diag_blocks_wide(kq[j][SUPER:]) * decay[j] for j in heads]

    l4 = [lmat[j] * w_base for j in heads]
    l4sq = [wdot(l4[j], l4[j]) for j in heads]
    l4cu = [wdot(l4[j], l4sq[j]) for j in heads]
    tinv = [w_eye - l4[j] + l4sq[j] - l4cu[j] for j in heads]
    for lvl in range(4):
        ct = [wdot(lmat[j] * wmask_ref[3 + lvl], tinv[j]) for j in heads]
        tinv = [tinv[j] - wdot(tinv[j], ct[j]) for j in heads]

    e_col = [jnp.exp(gc_col[j]) for j in heads]
    uw = [jnp.dot(block_diag(tinv[j]), jnp.concatenate([vb[j], kb[j] * e_col[j]], axis=1).astype(BF16),
                  preferred_element_type=F32) for j in heads]
    qg = [q[j] * e_col[j] for j in heads]
    kd_t = []
    for j in heads:
        gc_last_row = jnp.zeros((1, SUPER), F32)
        for c in range(SUPER // CHUNK):
            last = gc_row[j][:, (c + 1) * CHUNK - 1:(c + 1) * CHUNK]
            gc_last_row = jnp.where((lane_r >> 6) == c, last, gc_last_row)
        kd_t.append(k[j].T * jnp.exp(gc_last_row - gc_row[j]))

    ak = [jnp.concatenate([a_qk[j], kd_t[j]], axis=0).astype(BF16) for j in heads]
    state = [state_ref[j] for j in heads]
    outs = [[] for _ in heads]
    for c in range(n_chunks):
        r0 = c * CHUNK
        rows = slice(r0, r0 + CHUNK)
        ws = [_bdot(jnp.concatenate([uw[j][rows, LANE:], qg[j][rows]], axis=0), state[j]) for j in heads]
        v_new = [uw[j][rows, :LANE] - ws[j][:CHUNK] for j in heads]
        v_pad = [jnp.concatenate(([jnp.zeros((r0, LANE), BF16)] if r0 else []) + [v_new[j].astype(BF16)]
                                 + ([jnp.zeros((SUPER - r0 - CHUNK, LANE), BF16)] if r0 + CHUNK < SUPER else []),
                                 axis=0) for j in heads]
        both = [jnp.dot(ak[j], v_pad[j], preferred_element_type=F32) for j in heads]
        for j in heads:
            outs[j].append(ws[j][CHUNK:] + both[j][:CHUNK])
        state = [state[j] * jnp.exp(gc_row[j][:, r0 + CHUNK - 1:r0 + CHUNK]) + both[j][CHUNK:] for j in heads]
    for j in heads:
        state_ref[j] = state[j]
        o = jnp.concatenate(outs[j], axis=0)
        z = head_block(z_refs, j)
        o = o * lax.rsqrt(jnp.mean(o * o, axis=-1, keepdims=True) + RMS_EPS) * nw_ref[...] * (z * jax.nn.sigmoid(z))
        o_ref[:, sls[j]] = o.astype(o_ref.dtype)


def _gdn(proj, tail, tail_t, conv_w, alog_r, dtb_r, alog_c, dtb_c, norm_w, masks, batch, seq):
    T = proj.shape[0]
    ns = seq // SUPER
    hpb = GDN_HPB
    halo_per_super = SUPER // 8
    row = lambda b, i: b * ns + i
    halo_row = lambda b, i: jnp.maximum((b * ns + i) * halo_per_super - 1, 0)
    pieces = hpb // GDN_SPLIT
    piece_col = lambda col, h, m: col // GDN_SPLIT + h * pieces + m
    cur = lambda col: [pl.BlockSpec((SUPER, GDN_SPLIT * LANE), lambda b, h, i, m=m: (row(b, i), piece_col(col, h, m)))
                       for m in range(pieces)]
    halo = lambda col: [pl.BlockSpec((8, GDN_SPLIT * LANE), lambda b, h, i, m=m: (halo_row(b, i), piece_col(col, h, m)))
                        for m in range(pieces)]
    cw = lambda col: pl.BlockSpec((CONV_WIDTH, hpb * LANE), lambda b, h, i: (0, col // hpb + h))
    const2 = lambda shape: pl.BlockSpec(shape, lambda b, h, i: (0, 0))
    return pl.pallas_call(
        _gdn_kernel,
        grid=(batch, DN_HEADS // hpb, ns),
        in_specs=[*cur(DNQ_BLK), *cur(DNK_BLK), *cur(DNV_BLK),
                  *halo(DNQ_BLK), *halo(DNK_BLK), *halo(DNV_BLK), *cur(Z_BLK),
                  cw(0), cw(DN_HEADS), cw(2 * DN_HEADS),
                  pl.BlockSpec((SUPER, LANE), lambda b, h, i: (row(b, i), 0)),
                  pl.BlockSpec((16, SUPER), lambda b, h, i: (0, row(b, i))),
                  const2((1, LANE)), const2((1, LANE)), const2((16, 1)), const2((16, 1)),
                  const2((1, LANE)),
                  pl.BlockSpec(masks[0].shape, lambda b, h, i: (0, 0, 0)),
                  const2((SUPER, SUPER)), const2((SUPER, SUPER))],
        out_specs=pl.BlockSpec((SUPER, hpb * LANE), lambda b, h, i: (row(b, i), h)),
        out_shape=jax.ShapeDtypeStruct((T, DN_HEADS * DN_HEAD_DIM), BF16),
        scratch_shapes=[pltpu.VMEM((hpb, DN_HEAD_DIM, DN_HEAD_DIM), F32)],
        compiler_params=_params(("parallel", "parallel", "arbitrary"), 40),
        name="gdn",
    )(*([proj] * (7 * pieces)), conv_w, conv_w, conv_w, tail, tail_t,
      alog_r, dtb_r, alog_c, dtb_c, norm_w, *masks)


def _layer_norm(y, g, b):
    mu = jnp.mean(y, axis=-1, keepdims=True)
    yc = y - mu
    var = jnp.mean(yc * yc, axis=-1, keepdims=True)
    return yc * lax.rsqrt(var + LN_EPS) * g + b


def _outproj_kernel(alpha, a_ref, d_ref, wa_ref, wd_ref, x_ref, g_ref, b_ref, o_ref, ob_ref):
    mix = jnp.dot(a_ref[...], wa_ref[...], preferred_element_type=F32)
    mix = mix + jnp.dot(d_ref[...], wd_ref[...], preferred_element_type=F32)
    out = _layer_norm(alpha * x_ref[...] + mix, g_ref[...], b_ref[...])
    o_ref[...] = out
    ob_ref[...] = out.astype(BF16)


def _outproj_ln(attn, dn, w_out, x2d, g, b, l, alpha):
    T, D = x2d.shape
    tm = 256
    half = attn.shape[1]
    return pl.pallas_call(
        functools.partial(_outproj_kernel, alpha),
        grid=(T // tm,),
        in_specs=[pl.BlockSpec((tm, half), lambda i: (i, 0)),
                  pl.BlockSpec((tm, half), lambda i: (i, 0)),
                  pl.BlockSpec((None, half, D), lambda i: (l, 0, 0)),
                  pl.BlockSpec((None, half, D), lambda i: (l, 1, 0)),
                  pl.BlockSpec((tm, D), lambda i: (i, 0)),
                  pl.BlockSpec((None, 1, D), lambda i: (l, 0, 0)),
                  pl.BlockSpec((None, 1, D), lambda i: (l, 0, 0))],
        out_specs=[pl.BlockSpec((tm, D), lambda i: (i, 0))] * 2,
        out_shape=[jax.ShapeDtypeStruct((T, D), F32), jax.ShapeDtypeStruct((T, D), BF16)],
        compiler_params=_params(("parallel",), 48),
        name="outproj_ln",
    )(attn, dn, w_out, w_out, x2d, g, b)


def _router_kernel(x_ref, w_ref, bias_ref, gates_ref, grp_ref):
    xh, xl = _split_bf16(x_ref[...])
    w_parts = jnp.concatenate(_split_bf16(w_ref[...]), axis=1)
    both = jnp.dot(xh, w_parts, preferred_element_type=F32) + jnp.dot(xl, w_parts, preferred_element_type=F32)
    logits = both[:, :N_EXPERTS] + both[:, N_EXPERTS:]
    scores = jax.nn.sigmoid(logits)
    sel = scores + bias_ref[...]
    e = lax.broadcasted_iota(jnp.int32, sel.shape, 1)
    big = jnp.int32(N_EXPERTS)

    def top2(vals):
        m1 = jnp.max(vals, axis=-1, keepdims=True)
        i1 = jnp.min(jnp.where(vals == m1, e, big), axis=-1, keepdims=True)
        rest = jnp.where(e == i1, -jnp.inf, vals)
        m2 = jnp.max(rest, axis=-1, keepdims=True)
        i2 = jnp.min(jnp.where(rest == m2, e, big), axis=-1, keepdims=True)
        return m1, i1, m2, i2

    best = None
    for grp in range(N_GROUPS):
        m1, i1, m2, i2 = top2(jnp.where((e >> 2) == grp, sel, -jnp.inf))
        gsum = m1 + m2
        if best is None:
            best = (gsum, i1, i2)
        else:
            take = gsum > best[0]
            best = (jnp.where(take, gsum, best[0]), jnp.where(take, i1, best[1]), jnp.where(take, i2, best[2]))
    _, e1, e2 = best
    w1 = jnp.sum(jnp.where(e == e1, scores, 0.0), axis=-1, keepdims=True)
    w2 = jnp.sum(jnp.where(e == e2, scores, 0.0), axis=-1, keepdims=True)
    tot = w1 + w2
    gates_ref[...] = jnp.where(e == e1, w1 / tot, 0.0) + jnp.where(e == e2, w2 / tot, 0.0)
    grp_ref[...] = e1 >> 2


def _router(x2d, router_w, router_bias):
    T, D = x2d.shape
    tm = 512
    return pl.pallas_call(
        _router_kernel,
        grid=(T // tm,),
        in_specs=[pl.BlockSpec((tm, D), lambda i: (i, 0)),
                  pl.BlockSpec((D, N_EXPERTS), lambda i: (0, 0)),
                  pl.BlockSpec((1, N_EXPERTS), lambda i: (0, 0))],
        out_specs=[pl.BlockSpec((tm, N_EXPERTS), lambda i: (i, 0)), pl.BlockSpec((tm, 1), lambda i: (i, 0))],
        out_shape=[jax.ShapeDtypeStruct((T, N_EXPERTS), F32), jax.ShapeDtypeStruct((T, 1), jnp.int32)],
        compiler_params=_params(("parallel",), 32),
        name="router",
    )(x2d, router_w, router_bias.reshape(1, N_EXPERTS))


def _one_hot_rows(idx_col, width):
    lane = lax.broadcasted_iota(jnp.int32, (idx_col.shape[0], width), 1)
    return jnp.where(lane == idx_col, 1.0, 0.0).astype(BF16)


def _dispatch_kernel(idx_ref, xb_ref, gates_ref, xs_ref, gs_ref):
    p = _one_hot_rows(idx_ref[...], MOE_WINDOW)
    xs_ref[...] = jnp.dot(p, xb_ref[...], preferred_element_type=F32).astype(BF16)
    g_parts = jnp.concatenate(_split_bf16(gates_ref[...]), axis=1)
    both = jnp.dot(p, g_parts, preferred_element_type=F32)
    gs_ref[...] = both[:, :N_EXPERTS] + both[:, N_EXPERTS:]


def _dispatch(order_col, xb, gates):
    T, D = xb.shape
    tr = MOE_TM
    per = MOE_WINDOW // tr
    return pl.pallas_call(
        _dispatch_kernel,
        grid=(T // MOE_WINDOW, per),
        in_specs=[pl.BlockSpec((tr, 1), lambda w, j: (w * per + j, 0)),
                  pl.BlockSpec((MOE_WINDOW, D), lambda w, j: (w, 0)),
                  pl.BlockSpec((MOE_WINDOW, N_EXPERTS), lambda w, j: (w, 0))],
        out_specs=[pl.BlockSpec((tr, D), lambda w, j: (w * per + j, 0)),
                   pl.BlockSpec((tr, N_EXPERTS), lambda w, j: (w * per + j, 0))],
        out_shape=[jax.ShapeDtypeStruct((T, D), BF16), jax.ShapeDtypeStruct((T, N_EXPERTS), F32)],
        compiler_params=_params(("parallel", "arbitrary"), 48),
        name="moe_dispatch",
    )(order_col, xb, gates)


def _moe_kernel(act_ref, fetch_ref, x_ref, gates_ref, wg_ref, wu_ref, wd_ref, o_ref, acc_ref):
    tile, step = pl.program_id(0), pl.program_id(1)
    n_steps = N_EXPERTS // MOE_EPS

    @pl.when(step == 0)
    def _():
        acc_ref[...] = jnp.zeros_like(acc_ref)

    @pl.when(act_ref[tile * n_steps + step] > 0)
    def _():
        xb = x_ref[...]
        gates = gates_ref[...]
        lane = lax.broadcasted_iota(jnp.int32, gates.shape, 1)
        first_expert = fetch_ref[tile * n_steps + step] * MOE_EPS
        out = None
        for k in range(MOE_EPS):
            a = jnp.dot(xb, wg_ref[k], preferred_element_type=F32)
            u = jnp.dot(xb, wu_ref[k], preferred_element_type=F32)
            gate = jnp.sum(jnp.where(lane == first_expert + k, gates, 0.0), axis=-1, keepdims=True)
            hidden = (a * jax.nn.sigmoid(a)) * u * gate
            part = jnp.dot(hidden.astype(BF16), wd_ref[k], preferred_element_type=F32)
            out = part if out is None else out + part
        acc_ref[...] += out

    @pl.when(step == n_steps - 1)
    def _():
        o_ref[...] = acc_ref[...].astype(o_ref.dtype)


def _moe_ffn(active, fetch, xs, gates_s, w_gate, w_up, w_down, l):
    T, D = xs.shape
    tm = MOE_TM
    n_steps = N_EXPERTS // MOE_EPS
    wmap = lambda i, s, act, fetch: (l, fetch[i * n_steps + s], 0, 0)
    return pl.pallas_call(
        _moe_kernel,
        grid_spec=pltpu.PrefetchScalarGridSpec(
            num_scalar_prefetch=2,
            grid=(T // tm, n_steps),
            in_specs=[pl.BlockSpec((tm, D), lambda i, e, act, fetch: (i, 0)),
                      pl.BlockSpec((tm, N_EXPERTS), lambda i, e, act, fetch: (i, 0)),
                      pl.BlockSpec((None, MOE_EPS, D, D_EXPERT), wmap),
                      pl.BlockSpec((None, MOE_EPS, D, D_EXPERT), wmap),
                      pl.BlockSpec((None, MOE_EPS, D_EXPERT, D), wmap)],
            out_specs=pl.BlockSpec((tm, D), lambda i, e, act, fetch: (i, 0)),
            scratch_shapes=[pltpu.VMEM((tm, D), F32)]),
        out_shape=jax.ShapeDtypeStruct((T, D), BF16),
        compiler_params=_params(("parallel", "arbitrary"), 48),
        name="moe_ffn",
    )(active, fetch, xs, gates_s, w_gate, w_up, w_down)


def _combine_kernel(alpha, idx_ref, f_ref, x_ref, g_ref, b_ref, o_ref):
    p = _one_hot_rows(idx_ref[...], MOE_WINDOW)
    ffn = jnp.dot(p, f_ref[...], preferred_element_type=F32)
    o_ref[...] = _layer_norm(alpha * x_ref[...] + ffn, g_ref[...], b_ref[...])


def _combine_ln(dest_col, ffn_s, x2d, g, b, l, alpha):
    T, D = x2d.shape
    tr = MOE_TM
    per = MOE_WINDOW // tr
    return pl.pallas_call(
        functools.partial(_combine_kernel, alpha),
        grid=(T // MOE_WINDOW, per),
        in_specs=[pl.BlockSpec((tr, 1), lambda w, j: (w * per + j, 0)),
                  pl.BlockSpec((MOE_WINDOW, D), lambda w, j: (w, 0)),
                  pl.BlockSpec((tr, D), lambda w, j: (w * per + j, 0)),
                  pl.BlockSpec((None, 1, D), lambda w, j: (l, 0, 0)),
                  pl.BlockSpec((None, 1, D), lambda w, j: (l, 0, 0))],
        out_specs=pl.BlockSpec((tr, D), lambda w, j: (w * per + j, 0)),
        out_shape=jax.ShapeDtypeStruct((T, D), F32),
        compiler_params=_params(("parallel", "arbitrary"), 48),
        name="moe_combine_ln",
    )(dest_col, ffn_s, x2d, g, b)


def _dispatch_plan(grp, T):
    nw = T // MOE_WINDOW
    g2 = grp.reshape(nw, MOE_WINDOW)
    order = jnp.argsort(g2, axis=1, stable=True).astype(jnp.int32)
    dest = jnp.argsort(order, axis=1).astype(jnp.int32)
    grp_sorted = jnp.take_along_axis(g2, order, axis=1).reshape(T // MOE_TM, MOE_TM)
    present = (grp_sorted[:, :, None] == jnp.arange(N_GROUPS, dtype=jnp.int32)[None, None, :]).any(axis=1)
    n_steps = N_EXPERTS // MOE_EPS
    needed = jnp.repeat(present, EXPERTS_PER_GROUP // MOE_EPS, axis=1)
    blocks = jnp.argsort(jnp.logical_not(needed), axis=1, stable=True).astype(jnp.int32)
    count = needed.sum(axis=1).astype(jnp.int32)
    active = jnp.arange(n_steps, dtype=jnp.int32)[None, :] < count[:, None]
    last_block = jnp.take_along_axis(blocks, count[:, None] - 1, axis=1)[:, 0]
    next_first = jnp.concatenate([blocks[1:, 0], last_block[-1:]])
    fetch = jnp.where(active, blocks, next_first[:, None]).astype(jnp.int32)
    return order.reshape(T, 1), dest.reshape(T, 1), active.astype(jnp.int32).reshape(-1), fetch.reshape(-1)


def kernel(x, positions, w_in, attn_sinks, conv_w, a_log, dt_bias, dn_norm_w, w_out, ln1_g, ln1_b,
           router_w, router_bias, w_gate, w_up, w_down, ln2_g, ln2_b):
    batch, seq, d = x.shape
    depth = w_in.shape[0]
    T = batch * seq
    alpha = float((2 * depth) ** 0.25)

    w_main = w_in.astype(BF16)
    w_tail = jnp.concatenate(_split_bf16(
        jnp.pad(w_in[:, :, MAIN_COLS:], ((0, 0), (0, 0), (0, LANE - 2 * DN_HEADS)))), axis=-1)
    w_out_b = w_out.astype(BF16)
    w_gate_b, w_up_b, w_down_b = w_gate.astype(BF16), w_up.astype(BF16), w_down.astype(BF16)
    pad_heads = lambda t, n: jnp.pad(t, ((0, 0), (0, n - DN_HEADS)))
    alog_r = pad_heads(a_log, LANE).reshape(depth, 1, LANE)
    dtb_r = pad_heads(dt_bias, LANE).reshape(depth, 1, LANE)
    alog_c = pad_heads(a_log, 16).reshape(depth, 16, 1)
    dtb_c = pad_heads(dt_bias, 16).reshape(depth, 16, 1)
    masks = _delta_masks()
    ln1_g3, ln1_b3 = ln1_g.reshape(depth, 1, d), ln1_b.reshape(depth, 1, d)
    ln2_g3, ln2_b3 = ln2_g.reshape(depth, 1, d), ln2_b.reshape(depth, 1, d)

    cos, sin = _rope_tables(positions)
    h = x.reshape(T, d)
    for l in range(depth):
        proj, tail = _inproj(h, w_main, w_tail, l)
        tail_t = tail[:, :2 * DN_HEADS].T
        attn = _swa(proj, cos, sin, attn_sinks[l], batch, seq)
        dn = _gdn(proj, tail, tail_t, conv_w[l], alog_r[l], dtb_r[l], alog_c[l], dtb_c[l],
                  dn_norm_w[l].reshape(1, DN_HEAD_DIM), masks, batch, seq)
        h, hb = _outproj_ln(attn, dn, w_out_b, h, ln1_g3, ln1_b3, l, alpha)
        gates, grp = _router(h, router_w, router_bias)
        order_col, dest_col, active, fetch = _dispatch_plan(grp, T)
        xs, gates_s = _dispatch(order_col, hb, gates)
        ffn_s = _moe_ffn(active, fetch, xs, gates_s, w_gate_b, w_up_b, w_down_b, l)
        h = _combine_ln(dest_col, ffn_s, h, ln2_g3, ln2_b3, l, alpha)
    return h.reshape(batch, seq, d)
```
